```python
import math
import jax, jax.numpy as jnp
from jax import lax
import numpy as np

D_MODEL = 2048
BATCH = 2
SEQ = 8192
DEPTH = 4

MEM_LEN = 256
A_WIDTH = D_MODEL // 2
B_WIDTH = D_MODEL - A_WIDTH
DIFF_HEAD_DIM = 64
DIFF_HEADS = B_WIDTH // (2 * DIFF_HEAD_DIM)
CONF_KERNEL = 31
SHORT_KERNEL = 3
C_WIDTH = D_MODEL
X_HEADS = 4
X_HEAD_DIM = D_MODEL // X_HEADS
D_FF = 4 * D_MODEL
Q_BLOCK = 128
EPS = 1e-6
N_EVEN = (DEPTH + 1) // 2
N_ODD = DEPTH // 2
EVEN_IN = 2 * A_WIDTH + 3 * B_WIDTH
ODD_IN = 3 * C_WIDTH

kernel_name = "hybrid_conformer_diffattn_shortconv_trunk"


def rms_norm(x, g):
    x32 = x.astype(jnp.float32)
    y = x32 * lax.rsqrt(jnp.mean(jnp.square(x32), axis=-1, keepdims=True) + EPS)
    return (y * g.astype(jnp.float32)).astype(x.dtype)


def layer_norm(x, g, b):
    x32 = x.astype(jnp.float32)
    mu = jnp.mean(x32, axis=-1, keepdims=True)
    var = jnp.mean(jnp.square(x32 - mu), axis=-1, keepdims=True)
    y = (x32 - mu) * lax.rsqrt(var + EPS)
    return (y * g.astype(jnp.float32) + b.astype(jnp.float32)).astype(x.dtype)


def causal_dwconv(x, w):
    k_len, ch = w.shape
    return lax.conv_general_dilated(
        x, w[:, None, :].astype(x.dtype), window_strides=(1,), padding=[(k_len - 1, 0)],
        dimension_numbers=("NWC", "WIO", "NWC"), feature_group_count=ch)


def diff_attention(q, k, v, lam):
    bn, s_len, h, _, d = q.shape
    nb = s_len // Q_BLOCK
    qb = q.reshape(bn, nb, Q_BLOCK, h, 2, d).transpose(1, 0, 2, 3, 4, 5)
    k_pos = jnp.arange(s_len)
    scale = d ** -0.5

    def block(args):
        q_blk, i = args
        s = jnp.einsum('bqhcd,bkhcd->bhcqk', q_blk, k).astype(jnp.float32) * scale
        q_pos = i * Q_BLOCK + jnp.arange(Q_BLOCK)
        mask = k_pos[None, :] <= q_pos[:, None]
        s = jnp.where(mask, s, -jnp.inf)
        p = jax.nn.softmax(s, axis=-1)
        a = p[:, :, 0] - lam * p[:, :, 1]
        return jnp.einsum('bhqk,bkhe->bqhe', a.astype(v.dtype), v)

    o = lax.map(block, (qb, jnp.arange(nb)))
    return o.transpose(1, 0, 2, 3, 4).reshape(bn, s_len, h, 2 * d)


def even_mixer(h, w_in, conv_w, conv_b, ln_g, ln_b, lq1, lk1, lq2, lk2, subln_g, w_out, lambda_init):
    bn, s_len, _ = h.shape
    z = h @ w_in
    a_val, a_gate, q, k, v = jnp.split(
        z, [A_WIDTH, 2 * A_WIDTH, 2 * A_WIDTH + B_WIDTH, 2 * A_WIDTH + 2 * B_WIDTH], axis=-1)
    a = a_val * jax.nn.sigmoid(a_gate)
    a = causal_dwconv(a, conv_w) + conv_b
    a = jax.nn.silu(layer_norm(a, ln_g, ln_b))
    q = q.reshape(bn, s_len, DIFF_HEADS, 2, DIFF_HEAD_DIM)
    k = k.reshape(bn, s_len, DIFF_HEADS, 2, DIFF_HEAD_DIM)
    v = v.reshape(bn, s_len, DIFF_HEADS, 2 * DIFF_HEAD_DIM)
    f32 = jnp.float32
    lam = (jnp.exp(jnp.sum(lq1.astype(f32) * lk1.astype(f32)))
           - jnp.exp(jnp.sum(lq2.astype(f32) * lk2.astype(f32))) + lambda_init)
    o = diff_attention(q, k, v, lam)
    o = rms_norm(o, subln_g) * (1.0 - lambda_init)
    o = o.reshape(bn, s_len, B_WIDTH)
    return jnp.concatenate([a, o], axis=-1) @ w_out


def odd_mixer(h, w_in, conv_w, w_out):
    z = h @ w_in
    b_gate, c_gate, u = jnp.split(z, 3, axis=-1)
    y = b_gate * causal_dwconv(c_gate * u, conv_w)
    return y @ w_out


def cross_attention(h, mem_n, wq, wk, wv, wo):
    bn, s_len, _ = h.shape
    m_len = mem_n.shape[1]
    q = (h @ wq).reshape(bn, s_len, X_HEADS, X_HEAD_DIM)
    k = (mem_n @ wk).reshape(bn, m_len, X_HEADS, X_HEAD_DIM)
    v = (mem_n @ wv).reshape(bn, m_len, X_HEADS, X_HEAD_DIM)
    s = jnp.einsum('bshd,bmhd->bhsm', q, k).astype(jnp.float32) * (X_HEAD_DIM ** -0.5)
    p = jax.nn.softmax(s, axis=-1)
    o = jnp.einsum('bhsm,bmhd->bshd', p.astype(v.dtype), v).reshape(bn, s_len, D_MODEL)
    return o @ wo


def sq_relu_mlp(h, w1, w2):
    return jnp.square(jax.nn.relu(h @ w1)) @ w2


def setup_inputs(seed: int = 0) -> dict:
    key = jax.random.key(seed)
    ks = iter(jax.random.split(key, 32))
    f32 = jnp.float32

    def dense(shape):
        return jax.random.normal(next(ks), shape, f32) * (shape[-2] ** -0.5)

    def gain(shape):
        return 1.0 + 0.02 * jax.random.normal(next(ks), shape, f32)

    def small(shape, scale=0.02):
        return scale * jax.random.normal(next(ks), shape, f32)

    return {
        "x": jax.random.normal(next(ks), (BATCH, SEQ, D_MODEL), f32),
        "mem": jax.random.normal(next(ks), (BATCH, MEM_LEN, D_MODEL), f32),
        "mem_norm_g": gain((D_MODEL,)),
        "final_norm_g": gain((D_MODEL,)),
        "mix_norm_g": gain((DEPTH, D_MODEL)),
        "xattn_norm_g": gain((DEPTH, D_MODEL)),
        "mlp_norm_g": gain((DEPTH, D_MODEL)),
        "even_w_in": dense((N_EVEN, D_MODEL, EVEN_IN)),
        "conv_a_w": jax.random.normal(next(ks), (N_EVEN, CONF_KERNEL, A_WIDTH), f32) * (CONF_KERNEL ** -0.5),
        "conv_a_b": small((N_EVEN, A_WIDTH)),
        "ln_a_g": gain((N_EVEN, A_WIDTH)),
        "ln_a_b": small((N_EVEN, A_WIDTH)),
        "lambda_q1": small((N_EVEN, DIFF_HEAD_DIM), 0.1),
        "lambda_k1": small((N_EVEN, DIFF_HEAD_DIM), 0.1),
        "lambda_q2": small((N_EVEN, DIFF_HEAD_DIM), 0.1),
        "lambda_k2": small((N_EVEN, DIFF_HEAD_DIM), 0.1),
        "subln_g": gain((N_EVEN, 2 * DIFF_HEAD_DIM)),
        "even_w_out": dense((N_EVEN, D_MODEL, D_MODEL)),
        "odd_w_in": dense((N_ODD, D_MODEL, ODD_IN)),
        "conv_c_w": jax.random.normal(next(ks), (N_ODD, SHORT_KERNEL, C_WIDTH), f32) * (SHORT_KERNEL ** -0.5),
        "odd_w_out": dense((N_ODD, C_WIDTH, D_MODEL)),
        "xq_w": dense((DEPTH, D_MODEL, D_MODEL)),
        "xk_w": dense((DEPTH, D_MODEL, D_MODEL)),
        "xv_w": dense((DEPTH, D_MODEL, D_MODEL)),
        "xo_w": dense((DEPTH, D_MODEL, D_MODEL)),
        "mlp_w1": dense((DEPTH, D_MODEL, D_FF)),
        "mlp_w2": dense((DEPTH, D_FF, D_MODEL)),
    }


def reference(x, mem, mem_norm_g, final_norm_g, mix_norm_g, xattn_norm_g, mlp_norm_g,
              even_w_in, conv_a_w, conv_a_b, ln_a_g, ln_a_b, lambda_q1, lambda_k1,
              lambda_q2, lambda_k2, subln_g, even_w_out, odd_w_in, conv_c_w, odd_w_out,
              xq_w, xk_w, xv_w, xo_w, mlp_w1, mlp_w2):
    mem_n = rms_norm(mem, mem_norm_g)
    for l in range(DEPTH):
        hn = rms_norm(x, mix_norm_g[l])
        if l % 2 == 0:
            e = l // 2
            lambda_init = 0.8 - 0.6 * math.exp(-0.3 * l)
            x = x + even_mixer(hn, even_w_in[e], conv_a_w[e], conv_a_b[e], ln_a_g[e], ln_a_b[e],
                               lambda_q1[e], lambda_k1[e], lambda_q2[e], lambda_k2[e],
                               subln_g[e], even_w_out[e], lambda_init)
        else:
            o = l // 2
            x = x + odd_mixer(hn, odd_w_in[o], conv_c_w[o], odd_w_out[o])
        x = x + cross_attention(rms_norm(x, xattn_norm_g[l]), mem_n,
                                xq_w[l], xk_w[l], xv_w[l], xo_w[l])
        x = x + sq_relu_mlp(rms_norm(x, mlp_norm_g[l]), mlp_w1[l], mlp_w2[l])
    return rms_norm(x, final_norm_g)
```

```python
import functools

import numpy as np
import jax
import jax.numpy as jnp
from jax import lax
from jax.experimental import pallas as pl
from jax.experimental.pallas import tpu as pltpu

F32 = jnp.float32
BF16 = jnp.bfloat16
EPS = 1e-6
X_HEADS = 4
LANES = 128
SUBLANES = 8
VMEM_LIMIT_BYTES = 56 * 1024 * 1024
NT_DIMS = (((1,), (1,)), ((), ()))


def _params(semantics):
    return pltpu.CompilerParams(dimension_semantics=semantics, vmem_limit_bytes=VMEM_LIMIT_BYTES)


def _tile(n, target, quantum):
    t = min(n, target) // quantum * quantum
    while t > quantum and n % t:
        t -= quantum
    assert t > 0 and n % t == 0, (n, target, quantum)
    return t


def _rms_rows_to(x_ref, g_ref, dst_ref):
    n = x_ref.shape[0]
    rows = _tile(n, 256, SUBLANES)
    g = g_ref[...]

    def body(c, carry):
        r = pl.ds(pl.multiple_of(c * rows, rows), rows)
        x = x_ref[r, :].astype(F32)
        ms = jnp.mean(x * x, axis=-1, keepdims=True)
        dst_ref[r, :] = (x * lax.rsqrt(ms + EPS) * g).astype(dst_ref.dtype)
        return carry

    lax.fori_loop(0, n // rows, body, 0)


def _norm_matmul_kernel(x_ref, g_ref, w_ref, o_ref, xn_ref, *, group_major):
    @pl.when(pl.program_id(1) == 0)
    def _():
        _rms_rows_to(x_ref, g_ref, xn_ref)

    res = jnp.dot(xn_ref[...], w_ref[...], preferred_element_type=F32)
    if group_major:
        for g in range(o_ref.shape[0]):
            o_ref[g] = res[:, g * LANES:(g + 1) * LANES].astype(o_ref.dtype)
    else:
        o_ref[...] = res.astype(o_ref.dtype)


def norm_matmul(x, g, w, *, bm, bn, group_major=False, name):
    m, d = x.shape
    nl, _, n = w.shape
    nj = n // bn
    if group_major:
        out_shape = jax.ShapeDtypeStruct((nl * n // LANES, m, LANES), BF16)
        out_spec = pl.BlockSpec((bn // LANES, bm, LANES), lambda i, j: (j, i, 0))
    else:
        out_shape = jax.ShapeDtypeStruct((m, nl * n), BF16)
        out_spec = pl.BlockSpec((bm, bn), lambda i, j: (i, j))
    return pl.pallas_call(
        functools.partial(_norm_matmul_kernel, group_major=group_major),
        grid=(m // bm, nl * nj),
        in_specs=[
            pl.BlockSpec((bm, d), lambda i, j: (i, 0)),
            pl.BlockSpec((1, d), lambda i, j: (0, 0)),
            pl.BlockSpec((None, d, bn), lambda i, j: (j // nj, 0, j % nj)),
        ],
        out_specs=out_spec,
        out_shape=out_shape,
        scratch_shapes=[pltpu.VMEM((bm, d), BF16)],
        compiler_params=_params(("parallel", "arbitrary")),
        name=name,
    )(x, g, w)


def _conv_module_kernel(av_ref, ag_ref, avh_ref, agh_ref, cw_ref, cb_ref, lg_ref, lb_ref,
                        o_ref, g_scr, c_scr, *, tiles_per_seq):
    ga, ts, _ = av_ref.shape
    halo = avh_ref.shape[1]
    ksize = cw_ref.shape[1]
    rows = _tile(ts, 128, SUBLANES)
    keep = jnp.where(pl.program_id(0) % tiles_per_seq == 0, 0.0, 1.0).astype(F32)

    def glu(val, gate):
        return val.astype(F32) * jax.nn.sigmoid(gate.astype(F32))

    def conv_group(g, carry):
        g_scr[g, :halo, :] = glu(avh_ref[g], agh_ref[g]) * keep
        g_scr[g, halo:, :] = glu(av_ref[g], ag_ref[g])
        for r0 in range(0, ts, rows):
            acc = jnp.zeros((rows, LANES), F32)
            for j in range(ksize):
                off = r0 + halo - (ksize - 1) + j
                acc = acc + cw_ref[g, j:j + 1, :] * g_scr[g, off:off + rows, :]
            c_scr[g, r0:r0 + rows, :] = acc + cb_ref[g]
        return carry

    lax.fori_loop(0, ga, conv_group, 0)

    width = ga * LANES
    nrows = _tile(ts, 64, SUBLANES)

    def norm_rows(c, carry):
        r = pl.ds(pl.multiple_of(c * nrows, nrows), nrows)
        tot = c_scr[0, r, :]
        for g in range(1, ga):
            tot = tot + c_scr[g, r, :]
        mu = jnp.sum(tot, axis=-1, keepdims=True) / width
        sq = jnp.square(c_scr[0, r, :] - mu)
        for g in range(1, ga):
            sq = sq + jnp.square(c_scr[g, r, :] - mu)
        inv = lax.rsqrt(jnp.sum(sq, axis=-1, keepdims=True) / width + EPS)
        for g in range(ga):
            y = (c_scr[g, r, :] - mu) * inv * lg_ref[g] + lb_ref[g]
            o_ref[g, r, :] = (y * jax.nn.sigmoid(y)).astype(o_ref.dtype)
        return carry

    lax.fori_loop(0, ts // nrows, norm_rows, 0)


def conv_module(zg, conv_w, conv_b, ln_g, ln_b, *, seq, name):
    ksize, width = conv_w.shape
    ga = width // LANES
    t = zg.shape[1]
    ts = _tile(seq, 512, SUBLANES)
    halo = -(-(ksize - 1) // SUBLANES) * SUBLANES
    assert ts % halo == 0 and width % LANES == 0
    hb = ts // halo

    def per_group(v):
        return jnp.moveaxis(v.reshape(v.shape[:-1] + (ga, LANES)), -2, 0)

    cw = per_group(conv_w)
    cb, lg, lb = (per_group(v[None]) for v in (conv_b, ln_g, ln_b))
    full = lambda shape: pl.BlockSpec(shape, lambda i: (0,) * len(shape))
    return pl.pallas_call(
        functools.partial(_conv_module_kernel, tiles_per_seq=seq // ts),
        grid=(t // ts,),
        in_specs=[
            pl.BlockSpec((ga, ts, LANES), lambda i: (0, i, 0)),
            pl.BlockSpec((ga, ts, LANES), lambda i: (1, i, 0)),
            pl.BlockSpec((ga, halo, LANES), lambda i: (0, jnp.maximum(i * hb - 1, 0), 0)),
            pl.BlockSpec((ga, halo, LANES), lambda i: (1, jnp.maximum(i * hb - 1, 0), 0)),
            full((ga, ksize, LANES)), full((ga, 1, LANES)), full((ga, 1, LANES)), full((ga, 1, LANES)),
        ],
        out_specs=pl.BlockSpec((ga, ts, LANES), lambda i: (0, i, 0)),
        out_shape=jax.ShapeDtypeStruct((ga, t, LANES), BF16),
        scratch_shapes=[pltpu.VMEM((ga, halo + ts, LANES), F32), pltpu.VMEM((ga, ts, LANES), F32)],
        compiler_params=_params(("parallel",)),
        name=name,
    )(zg, zg, zg, zg, cw, cb, lg, lb)


def _diff_attn_kernel(qt_ref, kt_ref, q_ref, k_ref, v_ref, lq1_ref, lk1_ref, lq2_ref, lk2_ref, sg_ref,
                      o_ref, qs_scr, m_scr, l_scr, acc_scr, *, lambda_init):
    step_id = pl.program_id(1)
    qi = qt_ref[step_id]
    kj = kt_ref[step_id]
    heads, tq, _ = q_ref.shape
    tk = k_ref.shape[1]
    hd = LANES // 2
    scale = hd ** -0.5

    @pl.when(kj == 0)
    def _init():
        lane = lax.broadcasted_iota(jnp.int32, (tq, LANES), 1)

        def body(h, carry):
            q = (q_ref[h].astype(F32) * scale).astype(BF16)
            zero = jnp.zeros_like(q)
            qs_scr[h, :tq, :] = jnp.where(lane < hd, q, zero)
            qs_scr[h, tq:, :] = jnp.where(lane >= hd, q, zero)
            return carry

        lax.fori_loop(0, heads, body, 0)
        m_scr[...] = jnp.full(m_scr.shape, -jnp.inf, F32)
        l_scr[...] = jnp.zeros(l_scr.shape, F32)
        acc_scr[...] = jnp.zeros(acc_scr.shape, F32)

    def step(masked):
        def body(h, carry):
            s = lax.dot_general(qs_scr[h], k_ref[h], NT_DIMS, preferred_element_type=F32)
            if masked:
                row = lax.broadcasted_iota(jnp.int32, (2 * tq, tk), 0)
                col = lax.broadcasted_iota(jnp.int32, (2 * tq, tk), 1)
                s = jnp.where(col <= jnp.where(row >= tq, row - tq, row), s, -jnp.inf)
            m_prev = m_scr[h]
            m_new = jnp.maximum(m_prev, jnp.max(s, axis=-1, keepdims=True))
            alpha = jnp.exp(m_prev - m_new)
            p = jnp.exp(s - m_new)
            l_scr[h] = alpha * l_scr[h] + jnp.sum(p, axis=-1, keepdims=True)
            acc_scr[h] = alpha * acc_scr[h] + jnp.dot(p.astype(BF16), v_ref[h], preferred_element_type=F32)
            m_scr[h] = m_new
            return carry

        lax.fori_loop(0, heads, body, 0)

    @pl.when(kj < qi)
    def _off_diagonal():
        step(False)

    @pl.when(kj == qi)
    def _diagonal():
        step(True)
        lam = (jnp.exp(jnp.sum(lq1_ref[...] * lk1_ref[...], axis=-1, keepdims=True))
               - jnp.exp(jnp.sum(lq2_ref[...] * lk2_ref[...], axis=-1, keepdims=True)) + lambda_init)

        def finish(h, carry):
            o1 = acc_scr[h, :tq, :] / l_scr[h, :tq, :]
            o2 = acc_scr[h, tq:, :] / l_scr[h, tq:, :]
            o = o1 - lam * o2
            ms = jnp.mean(o * o, axis=-1, keepdims=True)
            y = o * lax.rsqrt(ms + EPS) * sg_ref[...]
            o_ref[h] = (y * (1.0 - lambda_init)).astype(o_ref.dtype)
            return carry

        lax.fori_loop(0, heads, finish, 0)


def diff_attention(zg, lq1, lk1, lq2, lk2, subln_g, *, batch, seq, a_width, heads, lambda_init, name):
    t = zg.shape[1]
    assert lq1.shape[-1] * 2 == LANES and subln_g.shape[-1] == LANES
    tq = _tile(seq, 512, SUBLANES)
    nq = seq // tq
    first_q = 2 * a_width // LANES
    assert first_q % heads == 0
    qb = first_q // heads
    pairs = [(i, j) for i in range(nq) for j in range(i + 1)]
    q_tab = jnp.asarray(np.array([p[0] for p in pairs], np.int32))
    k_tab = jnp.asarray(np.array([p[1] for p in pairs], np.int32))
    blk = (heads, tq, LANES)
    vec = lambda n: pl.BlockSpec((1, n), lambda b, s, qt, kt: (0, 0))
    grid_spec = pltpu.PrefetchScalarGridSpec(
        num_scalar_prefetch=2,
        grid=(batch, len(pairs)),
        in_specs=[
            pl.BlockSpec(blk, lambda b, s, qt, kt: (qb, b * nq + qt[s], 0)),
            pl.BlockSpec(blk, lambda b, s, qt, kt: (qb + 1, b * nq + kt[s], 0)),
            pl.BlockSpec(blk, lambda b, s, qt, kt: (qb + 2, b * nq + kt[s], 0)),
            vec(LANES // 2), vec(LANES // 2), vec(LANES // 2), vec(LANES // 2), vec(LANES),
        ],
        out_specs=pl.BlockSpec(blk, lambda b, s, qt, kt: (0, b * nq + qt[s], 0)),
        scratch_shapes=[
            pltpu.VMEM((heads, 2 * tq, LANES), BF16),
            pltpu.VMEM((heads, 2 * tq, 1), F32),
            pltpu.VMEM((heads, 2 * tq, 1), F32),
            pltpu.VMEM((heads, 2 * tq, LANES), F32),
        ],
    )
    return pl.pallas_call(
        functools.partial(_diff_attn_kernel, lambda_init=lambda_init),
        grid_spec=grid_spec,
        out_shape=jax.ShapeDtypeStruct((heads, t, LANES), BF16),
        compiler_params=_params(("parallel", "arbitrary")),
        name=name,
    )(q_tab, k_tab, zg, zg, zg, lq1[None], lk1[None], lq2[None], lk2[None], subln_g[None])


def _even_out_kernel(a_ref, b_ref, w_ref, r_ref, out_ref, lhs_scr):
    @pl.when(pl.program_id(1) == 0)
    def _():
        ga = a_ref.shape[0]
        for g in range(ga):
            lhs_scr[:, g * LANES:(g + 1) * LANES] = a_ref[g]
        for g in range(b_ref.shape[0]):
            lhs_scr[:, (ga + g) * LANES:(ga + g + 1) * LANES] = b_ref[g]

    out_ref[...] = r_ref[...] + jnp.dot(lhs_scr[...], w_ref[...], preferred_element_type=F32)


def even_out(a, b, w, resid, *, name):
    ga, m, _ = a.shape
    gb = b.shape[0]
    d, n = w.shape
    bm = _tile(m, 1024, SUBLANES)
    bn = _tile(n, 1024, LANES)
    return pl.pallas_call(
        _even_out_kernel,
        grid=(m // bm, n // bn),
        in_specs=[
            pl.BlockSpec((ga, bm, LANES), lambda i, j: (0, i, 0)),
            pl.BlockSpec((gb, bm, LANES), lambda i, j: (0, i, 0)),
            pl.BlockSpec((d, bn), lambda i, j: (0, j)),
            pl.BlockSpec((bm, bn), lambda i, j: (i, j)),
        ],
        out_specs=pl.BlockSpec((bm, bn), lambda i, j: (i, j)),
        out_shape=jax.ShapeDtypeStruct((m, n), F32),
        scratch_shapes=[pltpu.VMEM((bm, d), BF16)],
        compiler_params=_params(("parallel", "arbitrary")),
        name=name,
    )(a, b, w, resid)


def _odd_out_kernel(bg_ref, cg_ref, u_ref, cgh_ref, uh_ref, cw_ref, w_ref, r_ref, out_ref,
                    y_scr, cu_scr, *, tiles_per_seq):
    @pl.when(pl.program_id(1) == 0)
    def _():
        bm = bg_ref.shape[0]
        halo = cgh_ref.shape[0]
        ksize = cw_ref.shape[0]
        rows = _tile(bm, 128, SUBLANES)
        keep = jnp.where(pl.program_id(0) % tiles_per_seq == 0, 0.0, 1.0).astype(F32)
        cu_scr[:halo, :] = cgh_ref[...].astype(F32) * uh_ref[...].astype(F32) * keep
        for r0 in range(0, bm, rows):
            cu_scr[halo + r0:halo + r0 + rows, :] = (
                cg_ref[r0:r0 + rows, :].astype(F32) * u_ref[r0:r0 + rows, :].astype(F32))
        for r0 in range(0, bm, rows):
            acc = jnp.zeros((rows, cu_scr.shape[1]), F32)
            for j in range(ksize):
                off = r0 + halo - (ksize - 1) + j
                acc = acc + cw_ref[j:j + 1, :] * cu_scr[off:off + rows, :]
            y_scr[r0:r0 + rows, :] = (bg_ref[r0:r0 + rows, :].astype(F32) * acc).astype(y_scr.dtype)

    out_ref[...] = r_ref[...] + jnp.dot(y_scr[...], w_ref[...], preferred_element_type=F32)


def odd_out(z, conv_w, w, resid, *, seq, name):
    t = z.shape[0]
    ksize, c = conv_w.shape
    n = w.shape[1]
    bm = _tile(seq, 512, SUBLANES)
    bn = _tile(n, 1024, LANES)
    halo = -(-(ksize - 1) // SUBLANES) * SUBLANES
    hb = bm // halo
    halo_row = lambda i, j: jnp.maximum(i * hb - 1, 0)
    return pl.pallas_call(
        functools.partial(_odd_out_kernel, tiles_per_seq=seq // bm),
        grid=(t // bm, n // bn),
        in_specs=[
            pl.BlockSpec((bm, c), lambda i, j: (i, 0)),
            pl.BlockSpec((bm, c), lambda i, j: (i, 1)),
            pl.BlockSpec((bm, c), lambda i, j: (i, 2)),
            pl.BlockSpec((halo, c), lambda i, j: (halo_row(i, j), 1)),
            pl.BlockSpec((halo, c), lambda i, j: (halo_row(i, j), 2)),
            pl.BlockSpec((ksize, c), lambda i, j: (0, 0)),
            pl.BlockSpec((c, bn), lambda i, j: (0, j)),
            pl.BlockSpec((bm, bn), lambda i, j: (i, j)),
        ],
        out_specs=pl.BlockSpec((bm, bn), lambda i, j: (i, j)),
        out_shape=jax.ShapeDtypeStruct((t, n), F32),
        scratch_shapes=[pltpu.VMEM((bm, c), BF16), pltpu.VMEM((halo + bm, c), F32)],
        compiler_params=_params(("parallel", "arbitrary")),
        name=name,
    )(z, z, z, z, z, conv_w, w, resid)


def _xattn_out_kernel(q_ref, k_ref, v_ref, w_ref, r_ref, out_ref, o_scr, *, heads):
    @pl.when(pl.program_id(1) == 0)
    def _():
        hd = q_ref.shape[1] // heads
        scale = hd ** -0.5
        for h in range(heads):
            cols = slice(h * hd, (h + 1) * hd)
            s = lax.dot_general(q_ref[:, cols], k_ref[:, cols], NT_DIMS, preferred_element_type=F32) * scale
            e = jnp.exp(s - jnp.max(s, axis=-1, keepdims=True))
            p = e / jnp.sum(e, axis=-1, keepdims=True)
            o_scr[:, cols] = jnp.dot(p.astype(BF16), v_ref[:, cols],
                                     preferred_element_type=F32).astype(o_scr.dtype)

    out_ref[...] = r_ref[...] + jnp.dot(o_scr[...], w_ref[...], preferred_element_type=F32)


def xattn_out(q, kv, layer, w, resid, *, seq, mem_len, name):
    t, d = q.shape
    n = w.shape[1]
    bm = _tile(seq, 1024, SUBLANES)
    bn = _tile(n, 1024, LANES)
    tiles_per_seq = seq // bm
    return pl.pallas_call(
        functools.partial(_xattn_out_kernel, heads=X_HEADS),
        grid=(t // bm, n // bn),
        in_specs=[
            pl.BlockSpec((bm, d), lambda i, j: (i, 0)),
            pl.BlockSpec((mem_len, d), lambda i, j: (i // tiles_per_seq, 2 * layer)),
            pl.BlockSpec((mem_len, d), lambda i, j: (i // tiles_per_seq, 2 * layer + 1)),
            pl.BlockSpec((d, bn), lambda i, j: (0, j)),
            pl.BlockSpec((bm, bn), lambda i, j: (i, j)),
        ],
        out_specs=pl.BlockSpec((bm, bn), lambda i, j: (i, j)),
        out_shape=jax.ShapeDtypeStruct((t, n), F32),
        scratch_shapes=[pltpu.VMEM((bm, d), BF16)],
        compiler_params=_params(("parallel", "arbitrary")),
        name=name,
    )(q, kv, kv, w, resid)


def _mlp_kernel(x_ref, g_ref, w1_ref, w2_ref, fg_ref, o_ref, xn_ref, *, final_norm):
    j = pl.program_id(1)

    @pl.when(j == 0)
    def _():
        _rms_rows_to(x_ref, g_ref, xn_ref)
        o_ref[...] = x_ref[...]

    h = jnp.dot(xn_ref[...], w1_ref[...], preferred_element_type=F32)
    h = jnp.square(jnp.maximum(h, 0.0)).astype(BF16)
    o_ref[...] += jnp.dot(h, w2_ref[...], preferred_element_type=F32)

    if final_norm:
        @pl.when(j == pl.num_programs(1) - 1)
        def _():
            _rms_rows_to(o_ref, fg_ref, o_ref)


def mlp(x, g, w1, w2, final_g, *, final_norm, name):
    m, d = x.shape
    f = w1.shape[1]
    bm = _tile(m, 512, SUBLANES)
    tf = _tile(f, 1024, LANES)
    return pl.pallas_call(
        functools.partial(_mlp_kernel, final_norm=final_norm),
        grid=(m // bm, f // tf),
        in_specs=[
            pl.BlockSpec((bm, d), lambda i, j: (i, 0)),
            pl.BlockSpec((1, d), lambda i, j: (0, 0)),
            pl.BlockSpec((d, tf), lambda i, j: (0, j)),
            pl.BlockSpec((tf, d), lambda i, j: (j, 0)),
            pl.BlockSpec((1, d), lambda i, j: (0, 0)),
        ],
        out_specs=pl.BlockSpec((bm, d), lambda i, j: (i, 0)),
        out_shape=jax.ShapeDtypeStruct((m, d), F32),
        scratch_shapes=[pltpu.VMEM((bm, d), BF16)],
        compiler_params=_params(("parallel", "arbitrary")),
        name=name,
    )(x, g, w1, w2, final_g)


def kernel(x, mem, mem_norm_g, final_norm_g, mix_norm_g, xattn_norm_g, mlp_norm_g, even_w_in, conv_a_w, conv_a_b, ln_a_g, ln_a_b, lambda_q1, lambda_k1, lambda_q2, lambda_k2, subln_g, even_w_out, odd_w_in, conv_c_w, odd_w_out, xq_w, xk_w, xv_w, xo_w, mlp_w1, mlp_w2):
    batch, seq, d = x.shape
    mem_len = mem.shape[1]
    depth = xq_w.shape[0]
    a_width = conv_a_w.shape[-1]
    diff_heads = (d - a_width) // LANES
    t = batch * seq
    bf = lambda a: a.astype(BF16)
    row = lambda v: v[None].astype(F32)

    xf = x.reshape(t, d)
    bm = _tile(seq, 1024, SUBLANES)

    wkv = jnp.stack([bf(xk_w), bf(xv_w)], axis=1).reshape(2 * depth, d, d)
    kv = norm_matmul(mem.reshape(batch * mem_len, d), row(mem_norm_g), wkv,
                     bm=batch * mem_len, bn=_tile(d, 1024, LANES), name="mem_kv")

    for l in range(depth):
        if l % 2 == 0:
            e = l // 2
            lambda_init = 0.8 - 0.6 * float(np.exp(-0.3 * l))
            w_in = bf(even_w_in[e])[None]
            zg = norm_matmul(xf, row(mix_norm_g[l]), w_in, bm=bm, bn=_tile(w_in.shape[-1], 1024, LANES),
                             group_major=True, name=f"even_in_{l}")
            a = conv_module(zg, conv_a_w[e], conv_a_b[e], ln_a_g[e], ln_a_b[e], seq=seq, name=f"conv_module_{l}")
            o = diff_attention(zg, lambda_q1[e], lambda_k1[e], lambda_q2[e], lambda_k2[e], subln_g[e],
                               batch=batch, seq=seq, a_width=a_width, heads=diff_heads,
                               lambda_init=lambda_init, name=f"diff_attn_{l}")
            xf = even_out(a, o, bf(even_w_out[e]), xf, name=f"even_out_{l}")
        else:
            o = l // 2
            w_in = bf(odd_w_in[o])[None]
            z = norm_matmul(xf, row(mix_norm_g[l]), w_in, bm=bm, bn=_tile(w_in.shape[-1], 1024, LANES),
                            name=f"odd_in_{l}")
            xf = odd_out(z, conv_c_w[o], bf(odd_w_out[o]), xf, seq=seq, name=f"odd_out_{l}")
        q = norm_matmul(xf, row(xattn_norm_g[l]), bf(xq_w[l])[None], bm=bm, bn=_tile(d, 1024, LANES),
                        name=f"xattn_q_{l}")
        xf = xattn_out(q, kv, l, bf(xo_w[l]), xf, seq=seq, mem_len=mem_len, name=f"xattn_out_{l}")
        xf = mlp(xf, row(mlp_norm_g[l]), bf(mlp_w1[l]), bf(mlp_w2[l]), row(final_norm_g),
                 final_norm=(l == depth - 1), name=f"mlp_{l}")
    return xf.reshape(batch, seq, d)
```

```python
import functools

import numpy as np
import jax
import jax.numpy as jnp
from jax import lax
from jax.experimental import pallas as pl
from jax.experimental.pallas import tpu as pltpu

F32 = jnp.float32
BF16 = jnp.bfloat16
EPS = 1e-6
X_HEADS = 4
LANES = 128
SUBLANES = 8
VMEM_LIMIT_BYTES = 56 * 1024 * 1024
NT_DIMS = (((1,), (1,)), ((), ()))


def _params(semantics):
    return pltpu.CompilerParams(dimension_semantics=semantics, vmem_limit_bytes=VMEM_LIMIT_BYTES)


def _tile(n, target, quantum):
    t = min(n, target) // quantum * quantum
    while t > quantum and n % t:
        t -= quantum
    assert t > 0 and n % t == 0, (n, target, quantum)
    return t


def _rms_rows_to(x_ref, g_ref, dst_ref):
    n = x_ref.shape[0]
    rows = _tile(n, 256, SUBLANES)
    g = g_ref[...]

    def body(c, carry):
        r = pl.ds(pl.multiple_of(c * rows, rows), rows)
        x = x_ref[r, :].astype(F32)
        ms = jnp.mean(x * x, axis=-1, keepdims=True)
        dst_ref[r, :] = (x * lax.rsqrt(ms + EPS) * g).astype(dst_ref.dtype)
        return carry

    lax.fori_loop(0, n // rows, body, 0)


def _norm_matmul_kernel(x_ref, g_ref, w_ref, o_ref, xn_ref, *, group_major):
    @pl.when(pl.program_id(1) == 0)
    def _():
        _rms_rows_to(x_ref, g_ref, xn_ref)

    res = jnp.dot(xn_ref[...], w_ref[...], preferred_element_type=F32)
    if group_major:
        for g in range(o_ref.shape[0]):
            o_ref[g] = res[:, g * LANES:(g + 1) * LANES].astype(o_ref.dtype)
    else:
        o_ref[...] = res.astype(o_ref.dtype)


def norm_matmul(x, g, w, *, layers, bm, bn, group_major=False, name):
    m, d = x.shape
    n = w.shape[2]
    first, nl = layers
    nj = n // bn
    if group_major:
        out_shape = jax.ShapeDtypeStruct((nl * n // LANES, m, LANES), BF16)
        out_spec = pl.BlockSpec((bn // LANES, bm, LANES), lambda i, j: (j, i, 0))
    else:
        out_shape = jax.ShapeDtypeStruct((m, nl * n), BF16)
        out_spec = pl.BlockSpec((bm, bn), lambda i, j: (i, j))
    return pl.pallas_call(
        functools.partial(_norm_matmul_kernel, group_major=group_major),
        grid=(m // bm, nl * nj),
        in_specs=[
            pl.BlockSpec((bm, d), lambda i, j: (i, 0)),
            pl.BlockSpec((1, d), lambda i, j: (0, 0)),
            pl.BlockSpec((None, d, bn), lambda i, j: (first + j // nj, 0, j % nj)),
        ],
        out_specs=out_spec,
        out_shape=out_shape,
        scratch_shapes=[pltpu.VMEM((bm, d), BF16)],
        compiler_params=_params(("parallel", "arbitrary")),
        name=name,
    )(x, g, w)


def _conv_module_kernel(av_ref, ag_ref, avh_ref, agh_ref, cw_ref, cb_ref, lg_ref, lb_ref,
                        o_ref, g_scr, c_scr, *, tiles_per_seq):
    ga, ts, _ = av_ref.shape
    halo = avh_ref.shape[1]
    ksize = cw_ref.shape[1]
    rows = _tile(ts, 128, SUBLANES)
    keep = jnp.where(pl.program_id(0) % tiles_per_seq == 0, 0.0, 1.0).astype(F32)

    def glu(val, gate):
        return val.astype(F32) * jax.nn.sigmoid(gate.astype(F32))

    def conv_group(g, carry):
        g_scr[g, :halo, :] = glu(avh_ref[g], agh_ref[g]) * keep
        g_scr[g, halo:, :] = glu(av_ref[g], ag_ref[g])
        for r0 in range(0, ts, rows):
            acc = jnp.zeros((rows, LANES), F32)
            for j in range(ksize):
                off = r0 + halo - (ksize - 1) + j
                acc = acc + cw_ref[g, j:j + 1, :] * g_scr[g, off:off + rows, :]
            c_scr[g, r0:r0 + rows, :] = acc + cb_ref[g]
        return carry

    lax.fori_loop(0, ga, conv_group, 0)

    width = ga * LANES
    nrows = _tile(ts, 64, SUBLANES)

    def norm_rows(c, carry):
        r = pl.ds(pl.multiple_of(c * nrows, nrows), nrows)
        tot = c_scr[0, r, :]
        for g in range(1, ga):
            tot = tot + c_scr[g, r, :]
        mu = jnp.sum(tot, axis=-1, keepdims=True) / width
        sq = jnp.square(c_scr[0, r, :] - mu)
        for g in range(1, ga):
            sq = sq + jnp.square(c_scr[g, r, :] - mu)
        inv = lax.rsqrt(jnp.sum(sq, axis=-1, keepdims=True) / width + EPS)
        for g in range(ga):
            y = (c_scr[g, r, :] - mu) * inv * lg_ref[g] + lb_ref[g]
            o_ref[g, r, :] = (y * jax.nn.sigmoid(y)).astype(o_ref.dtype)
        return carry

    lax.fori_loop(0, ts // nrows, norm_rows, 0)


def conv_module(zg, conv_w, conv_b, ln_g, ln_b, *, seq, name):
    ksize, width = conv_w.shape
    ga = width // LANES
    t = zg.shape[1]
    ts = _tile(seq, 512, SUBLANES)
    halo = -(-(ksize - 1) // SUBLANES) * SUBLANES
    assert ts % halo == 0 and width % LANES == 0
    hb = ts // halo

    def per_group(v):
        return jnp.moveaxis(v.reshape(v.shape[:-1] + (ga, LANES)), -2, 0)

    cw = per_group(conv_w)
    cb, lg, lb = (per_group(v[None]) for v in (conv_b, ln_g, ln_b))
    full = lambda shape: pl.BlockSpec(shape, lambda i: (0,) * len(shape))
    return pl.pallas_call(
        functools.partial(_conv_module_kernel, tiles_per_seq=seq // ts),
        grid=(t // ts,),
        in_specs=[
            pl.BlockSpec((ga, ts, LANES), lambda i: (0, i, 0)),
            pl.BlockSpec((ga, ts, LANES), lambda i: (1, i, 0)),
            pl.BlockSpec((ga, halo, LANES), lambda i: (0, jnp.maximum(i * hb - 1, 0), 0)),
            pl.BlockSpec((ga, halo, LANES), lambda i: (1, jnp.maximum(i * hb - 1, 0), 0)),
            full((ga, ksize, LANES)), full((ga, 1, LANES)), full((ga, 1, LANES)), full((ga, 1, LANES)),
        ],
        out_specs=pl.BlockSpec((ga, ts, LANES), lambda i: (0, i, 0)),
        out_shape=jax.ShapeDtypeStruct((ga, t, LANES), BF16),
        scratch_shapes=[pltpu.VMEM((ga, halo + ts, LANES), F32), pltpu.VMEM((ga, ts, LANES), F32)],
        compiler_params=_params(("parallel",)),
        name=name,
    )(zg, zg, zg, zg, cw, cb, lg, lb)


def _diff_attn_kernel(qt_ref, kt_ref, q_ref, k_ref, v_ref, lq1_ref, lk1_ref, lq2_ref, lk2_ref, sg_ref,
                      o_ref, qs_scr, vx_scr, m_scr, acc_scr, *, lambda_init, chunk):
    step_id = pl.program_id(1)
    qi = qt_ref[step_id]
    kj = kt_ref[step_id]
    heads, tq, _ = q_ref.shape
    tk = k_ref.shape[1]
    hd = LANES // 2
    scale = hd ** -0.5

    @pl.when(kj == 0)
    def _init():
        lane = lax.broadcasted_iota(jnp.int32, (tq, LANES), 1)
        ones_col = jnp.where(lax.broadcasted_iota(jnp.int32, (tk, LANES), 1) == 0, 1.0, 0.0).astype(BF16)

        def body(h, carry):
            q = (q_ref[h].astype(F32) * scale).astype(BF16)
            zero = jnp.zeros_like(q)
            qs_scr[h, :tq, :] = jnp.where(lane < hd, q, zero)
            qs_scr[h, tq:, :] = jnp.where(lane >= hd, q, zero)
            vx_scr[h, :, LANES:] = ones_col
            return carry

        lax.fori_loop(0, heads, body, 0)
        m_scr[...] = jnp.full(m_scr.shape, -jnp.inf, F32)
        acc_scr[...] = jnp.zeros(acc_scr.shape, F32)

    def step(masked):
        def body(h, carry):
            vx_scr[h, :, :LANES] = v_ref[h]
            k = k_ref[h]
            vx = vx_scr[h]
            for r0 in range(0, 2 * tq, chunk):
                rows = slice(r0, r0 + chunk)
                s = lax.dot_general(qs_scr[h, rows, :], k, NT_DIMS, preferred_element_type=F32)
                if masked:
                    row = lax.broadcasted_iota(jnp.int32, (chunk, tk), 0) + (r0 % tq)
                    col = lax.broadcasted_iota(jnp.int32, (chunk, tk), 1)
                    s = jnp.where(col <= row, s, -jnp.inf)
                m_prev = m_scr[h, rows, :]
                m_new = jnp.maximum(m_prev, jnp.max(s, axis=-1, keepdims=True))
                alpha = jnp.exp(m_prev - m_new)
                p = jnp.exp(s - jnp.concatenate([m_new] * (tk // LANES), axis=1))
                pv = jnp.dot(p.astype(BF16), vx, preferred_element_type=F32)
                acc_scr[h, rows, :] = jnp.concatenate([alpha, alpha], axis=1) * acc_scr[h, rows, :] + pv
                m_scr[h, rows, :] = m_new
            return carry

        lax.fori_loop(0, heads, body, 0, unroll=True)

    @pl.when(kj < qi)
    def _off_diagonal():
        step(False)

    @pl.when(kj == qi)
    def _diagonal():
        step(True)
        lam = (jnp.exp(jnp.sum(lq1_ref[...] * lk1_ref[...], axis=-1, keepdims=True))
               - jnp.exp(jnp.sum(lq2_ref[...] * lk2_ref[...], axis=-1, keepdims=True)) + lambda_init)

        def finish(h, carry):
            o1 = acc_scr[h, :tq, :LANES] / acc_scr[h, :tq, LANES:LANES + 1]
            o2 = acc_scr[h, tq:, :LANES] / acc_scr[h, tq:, LANES:LANES + 1]
            o = o1 - lam * o2
            ms = jnp.mean(o * o, axis=-1, keepdims=True)
            y = o * lax.rsqrt(ms + EPS) * sg_ref[...]
            o_ref[h] = (y * (1.0 - lambda_init)).astype(o_ref.dtype)
            return carry

        lax.fori_loop(0, heads, finish, 0)


def diff_attention(zg, lq1, lk1, lq2, lk2, subln_g, *, batch, seq, a_width, heads, lambda_init, name):
    t = zg.shape[1]
    assert lq1.shape[-1] * 2 == LANES and subln_g.shape[-1] == LANES
    tq = _tile(seq, 512, SUBLANES)
    nq = seq // tq
    first_q = 2 * a_width // LANES
    assert first_q % heads == 0
    qb = first_q // heads
    pairs = [(i, j) for i in range(nq) for j in range(i + 1)]
    q_tab = jnp.asarray(np.array([p[0] for p in pairs], np.int32))
    k_tab = jnp.asarray(np.array([p[1] for p in pairs], np.int32))
    blk = (heads, tq, LANES)
    vec = lambda n: pl.BlockSpec((1, n), lambda b, s, qt, kt: (0, 0))
    grid_spec = pltpu.PrefetchScalarGridSpec(
        num_scalar_prefetch=2,
        grid=(batch, len(pairs)),
        in_specs=[
            pl.BlockSpec(blk, lambda b, s, qt, kt: (qb, b * nq + qt[s], 0)),
            pl.BlockSpec(blk, lambda b, s, qt, kt: (qb + 1, b * nq + kt[s], 0)),
            pl.BlockSpec(blk, lambda b, s, qt, kt: (qb + 2, b * nq + kt[s], 0)),
            vec(LANES // 2), vec(LANES // 2), vec(LANES // 2), vec(LANES // 2), vec(LANES),
        ],
        out_specs=pl.BlockSpec(blk, lambda b, s, qt, kt: (0, b * nq + qt[s], 0)),
        scratch_shapes=[
            pltpu.VMEM((heads, 2 * tq, LANES), BF16),
            pltpu.VMEM((heads, tq, 2 * LANES), BF16),
            pltpu.VMEM((heads, 2 * tq, LANES), F32),
            pltpu.VMEM((heads, 2 * tq, 2 * LANES), F32),
        ],
    )
    return pl.pallas_call(
        functools.partial(_diff_attn_kernel, lambda_init=lambda_init, chunk=_tile(tq, 256, SUBLANES)),
        grid_spec=grid_spec,
        out_shape=jax.ShapeDtypeStruct((heads, t, LANES), BF16),
        compiler_params=_params(("parallel", "arbitrary")),
        name=name,
    )(q_tab, k_tab, zg, zg, zg, lq1[None], lk1[None], lq2[None], lk2[None], subln_g[None])


def _even_out_kernel(a_ref, b_ref, w_ref, r_ref, out_ref, lhs_scr):
    @pl.when(pl.program_id(1) == 0)
    def _():
        ga = a_ref.shape[0]
        for g in range(ga):
            lhs_scr[:, g * LANES:(g + 1) * LANES] = a_ref[g]
        for g in range(b_ref.shape[0]):
            lhs_scr[:, (ga + g) * LANES:(ga + g + 1) * LANES] = b_ref[g]

    out_ref[...] = r_ref[...] + jnp.dot(lhs_scr[...], w_ref[...], preferred_element_type=F32)


def even_out(a, b, w, layer, resid, *, name):
    ga, m, _ = a.shape
    gb = b.shape[0]
    _, d, n = w.shape
    bm = _tile(m, 1024, SUBLANES)
    bn = _tile(n, 1024, LANES)
    return pl.pallas_call(
        _even_out_kernel,
        grid=(m // bm, n // bn),
        in_specs=[
            pl.BlockSpec((ga, bm, LANES), lambda i, j: (0, i, 0)),
            pl.BlockSpec((gb, bm, LANES), lambda i, j: (0, i, 0)),
            pl.BlockSpec((None, d, bn), lambda i, j: (layer, 0, j)),
            pl.BlockSpec((bm, bn), lambda i, j: (i, j)),
        ],
        out_specs=pl.BlockSpec((bm, bn), lambda i, j: (i, j)),
        out_shape=jax.ShapeDtypeStruct((m, n), F32),
        scratch_shapes=[pltpu.VMEM((bm, d), BF16)],
        compiler_params=_params(("parallel", "arbitrary")),
        name=name,
    )(a, b, w, resid)


def _odd_out_kernel(bg_ref, cg_ref, u_ref, cgh_ref, uh_ref, cw_ref, w_ref, r_ref, out_ref,
                    y_scr, cu_scr, *, tiles_per_seq):
    @pl.when(pl.program_id(1) == 0)
    def _():
        bm = bg_ref.shape[0]
        halo = cgh_ref.shape[0]
        ksize = cw_ref.shape[0]
        rows = _tile(bm, 128, SUBLANES)
        keep = jnp.where(pl.program_id(0) % tiles_per_seq == 0, 0.0, 1.0).astype(F32)
        cu_scr[:halo, :] = cgh_ref[...].astype(F32) * uh_ref[...].astype(F32) * keep
        for r0 in range(0, bm, rows):
            cu_scr[halo + r0:halo + r0 + rows, :] = (
                cg_ref[r0:r0 + rows, :].astype(F32) * u_ref[r0:r0 + rows, :].astype(F32))
        for r0 in range(0, bm, rows):
            acc = jnp.zeros((rows, cu_scr.shape[1]), F32)
            for j in range(ksize):
                off = r0 + halo - (ksize - 1) + j
                acc = acc + cw_ref[j:j + 1, :] * cu_scr[off:off + rows, :]
            y_scr[r0:r0 + rows, :] = (bg_ref[r0:r0 + rows, :].astype(F32) * acc).astype(y_scr.dtype)

    out_ref[...] = r_ref[...] + jnp.dot(y_scr[...], w_ref[...], preferred_element_type=F32)


def odd_out(z, conv_w, w, layer, resid, *, seq, name):
    t = z.shape[0]
    ksize, c = conv_w.shape
    n = w.shape[2]
    bm = _tile(seq, 512, SUBLANES)
    bn = _tile(n, 1024, LANES)
    halo = -(-(ksize - 1) // SUBLANES) * SUBLANES
    hb = bm // halo
    halo_row = lambda i, j: jnp.maximum(i * hb - 1, 0)
    return pl.pallas_call(
        functools.partial(_odd_out_kernel, tiles_per_seq=seq // bm),
        grid=(t // bm, n // bn),
        in_specs=[
            pl.BlockSpec((bm, c), lambda i, j: (i, 0)),
            pl.BlockSpec((bm, c), lambda i, j: (i, 1)),
            pl.BlockSpec((bm, c), lambda i, j: (i, 2)),
            pl.BlockSpec((halo, c), lambda i, j: (halo_row(i, j), 1)),
            pl.BlockSpec((halo, c), lambda i, j: (halo_row(i, j), 2)),
            pl.BlockSpec((ksize, c), lambda i, j: (0, 0)),
            pl.BlockSpec((None, c, bn), lambda i, j: (layer, 0, j)),
            pl.BlockSpec((bm, bn), lambda i, j: (i, j)),
        ],
        out_specs=pl.BlockSpec((bm, bn), lambda i, j: (i, j)),
        out_shape=jax.ShapeDtypeStruct((t, n), F32),
        scratch_shapes=[pltpu.VMEM((bm, c), BF16), pltpu.VMEM((halo + bm, c), F32)],
        compiler_params=_params(("parallel", "arbitrary")),
        name=name,
    )(z, z, z, z, z, conv_w, w, resid)


def _xattn_out_kernel(q_ref, k_ref, v_ref, w_ref, r_ref, out_ref, o_scr, *, heads):
    @pl.when(pl.program_id(1) == 0)
    def _():
        hd = q_ref.shape[1] // heads
        scale = hd ** -0.5
        for h in range(heads):
            cols = slice(h * hd, (h + 1) * hd)
            s = lax.dot_general(q_ref[:, cols], k_ref[:, cols], NT_DIMS, preferred_element_type=F32) * scale
            e = jnp.exp(s - jnp.max(s, axis=-1, keepdims=True))
            p = e / jnp.sum(e, axis=-1, keepdims=True)
            o_scr[:, cols] = jnp.dot(p.astype(BF16), v_ref[:, cols],
                                     preferred_element_type=F32).astype(o_scr.dtype)

    out_ref[...] = r_ref[...] + jnp.dot(o_scr[...], w_ref[...], preferred_element_type=F32)


def xattn_out(q, k_all, v_all, w, layer, resid, *, seq, mem_len, name):
    t, d = q.shape
    n = w.shape[2]
    bm = _tile(seq, 1024, SUBLANES)
    bn = _tile(n, 1024, LANES)
    tiles_per_seq = seq // bm
    return pl.pallas_call(
        functools.partial(_xattn_out_kernel, heads=X_HEADS),
        grid=(t // bm, n // bn),
        in_specs=[
            pl.BlockSpec((bm, d), lambda i, j: (i, 0)),
            pl.BlockSpec((mem_len, d), lambda i, j: (i // tiles_per_seq, layer)),
            pl.BlockSpec((mem_len, d), lambda i, j: (i // tiles_per_seq, layer)),
            pl.BlockSpec((None, d, bn), lambda i, j: (layer, 0, j)),
            pl.BlockSpec((bm, bn), lambda i, j: (i, j)),
        ],
        out_specs=pl.BlockSpec((bm, bn), lambda i, j: (i, j)),
        out_shape=jax.ShapeDtypeStruct((t, n), F32),
        scratch_shapes=[pltpu.VMEM((bm, d), BF16)],
        compiler_params=_params(("parallel", "arbitrary")),
        name=name,
    )(q, k_all, v_all, w, resid)


def _mlp_kernel(x_ref, g_ref, w1_ref, w2_ref, fg_ref, o_ref, xn_ref, *, final_norm):
    j = pl.program_id(1)

    @pl.when(j == 0)
    def _():
        _rms_rows_to(x_ref, g_ref, xn_ref)
        o_ref[...] = x_ref[...]

    h = jnp.dot(xn_ref[...], w1_ref[...], preferred_element_type=F32)
    h = jnp.square(jnp.maximum(h, 0.0)).astype(BF16)
    o_ref[...] += jnp.dot(h, w2_ref[...], preferred_element_type=F32)

    if final_norm:
        @pl.when(j == pl.num_programs(1) - 1)
        def _():
            _rms_rows_to(o_ref, fg_ref, o_ref)


def mlp(x, g, w1, w2, layer, final_g, *, final_norm, name):
    m, d = x.shape
    f = w1.shape[2]
    bm = _tile(m, 512, SUBLANES)
    tf = _tile(f, 1024, LANES)
    return pl.pallas_call(
        functools.partial(_mlp_kernel, final_norm=final_norm),
        grid=(m // bm, f // tf),
        in_specs=[
            pl.BlockSpec((bm, d), lambda i, j: (i, 0)),
            pl.BlockSpec((1, d), lambda i, j: (0, 0)),
            pl.BlockSpec((None, d, tf), lambda i, j: (layer, 0, j)),
            pl.BlockSpec((None, tf, d), lambda i, j: (layer, j, 0)),
            pl.BlockSpec((1, d), lambda i, j: (0, 0)),
        ],
        out_specs=pl.BlockSpec((bm, d), lambda i, j: (i, 0)),
        out_shape=jax.ShapeDtypeStruct((m, d), F32),
        scratch_shapes=[pltpu.VMEM((bm, d), BF16)],
        compiler_params=_params(("parallel", "arbitrary")),
        name=name,
    )(x, g, w1, w2, final_g)


def kernel(x, mem, mem_norm_g, final_norm_g, mix_norm_g, xattn_norm_g, mlp_norm_g, even_w_in, conv_a_w, conv_a_b, ln_a_g, ln_a_b, lambda_q1, lambda_k1, lambda_q2, lambda_k2, subln_g, even_w_out, odd_w_in, conv_c_w, odd_w_out, xq_w, xk_w, xv_w, xo_w, mlp_w1, mlp_w2):
    batch, seq, d = x.shape
    mem_len = mem.shape[1]
    depth = xq_w.shape[0]
    a_width = conv_a_w.shape[-1]
    diff_heads = (d - a_width) // LANES
    t = batch * seq
    bf = lambda a: a.astype(BF16)
    row = lambda v: v[None].astype(F32)

    xf = x.reshape(t, d)
    bm = _tile(seq, 1024, SUBLANES)

    even_w_in, even_w_out, odd_w_in, odd_w_out, xq_w, xk_w, xv_w, xo_w, mlp_w1, mlp_w2 = (
        bf(w) for w in (even_w_in, even_w_out, odd_w_in, odd_w_out, xq_w, xk_w, xv_w, xo_w, mlp_w1, mlp_w2))
    bn_d = _tile(d, 1024, LANES)

    mem_f = mem.reshape(batch * mem_len, d)
    k_all = norm_matmul(mem_f, row(mem_norm_g), xk_w, layers=(0, depth), bm=batch * mem_len, bn=bn_d, name="mem_k")
    v_all = norm_matmul(mem_f, row(mem_norm_g), xv_w, layers=(0, depth), bm=batch * mem_len, bn=bn_d, name="mem_v")

    for l in range(depth):
        if l % 2 == 0:
            e = l // 2
            lambda_init = 0.8 - 0.6 * float(np.exp(-0.3 * l))
            zg = norm_matmul(xf, row(mix_norm_g[l]), even_w_in, layers=(e, 1), bm=bm,
                             bn=_tile(even_w_in.shape[-1], 1024, LANES), group_major=True, name=f"even_in_{l}")
            a = conv_module(zg, conv_a_w[e], conv_a_b[e], ln_a_g[e], ln_a_b[e], seq=seq, name=f"conv_module_{l}")
            o = diff_attention(zg, lambda_q1[e], lambda_k1[e], lambda_q2[e], lambda_k2[e], subln_g[e],
                               batch=batch, seq=seq, a_width=a_width, heads=diff_heads,
                               lambda_init=lambda_init, name=f"diff_attn_{l}")
            xf = even_out(a, o, even_w_out, e, xf, name=f"even_out_{l}")
        else:
            o = l // 2
            z = norm_matmul(xf, row(mix_norm_g[l]), odd_w_in, layers=(o, 1), bm=bm,
                            bn=_tile(odd_w_in.shape[-1], 1024, LANES), name=f"odd_in_{l}")
            xf = odd_out(z, conv_c_w[o], odd_w_out, o, xf, seq=seq, name=f"odd_out_{l}")
        q = norm_matmul(xf, row(xattn_norm_g[l]), xq_w, layers=(l, 1), bm=bm, bn=bn_d, name=f"xattn_q_{l}")
        xf = xattn_out(q, k_all, v_all, xo_w, l, xf, seq=seq, mem_len=mem_len, name=f"xattn_out_{l}")
        xf = mlp(xf, row(mlp_norm_g[l]), mlp_w1, mlp_w2, l, row(final_norm_g),
                 final_norm=(l == depth - 1), name=f"mlp_{l}")
    return xf.reshape(batch, seq, d)
```

```python
import functools

import numpy as np
import jax
import jax.numpy as jnp
from jax import lax
from jax.experimental import pallas as pl
from jax.experimental.pallas import tpu as pltpu

F32 = jnp.float32
BF16 = jnp.bfloat16
EPS = 1e-6
X_HEADS = 4
LANES = 128
SUBLANES = 8
VMEM_LIMIT_BYTES = 56 * 1024 * 1024
NT_DIMS = (((1,), (1,)), ((), ()))


def _params(semantics):
    return pltpu.CompilerParams(dimension_semantics=semantics, vmem_limit_bytes=VMEM_LIMIT_BYTES)


def _tile(n, target, quantum):
    t = min(n, target) // quantum * quantum
    while t > quantum and n % t:
        t -= quantum
    assert t > 0 and n % t == 0, (n, target, quantum)
    return t


def _rms_rows_to(x_ref, g_ref, dst_ref):
    n = x_ref.shape[0]
    rows = _tile(n, 256, SUBLANES)
    g = g_ref[...]

    def body(c, carry):
        r = pl.ds(pl.multiple_of(c * rows, rows), rows)
        x = x_ref[r, :].astype(F32)
        ms = jnp.mean(x * x, axis=-1, keepdims=True)
        dst_ref[r, :] = (x * lax.rsqrt(ms + EPS) * g).astype(dst_ref.dtype)
        return carry

    lax.fori_loop(0, n // rows, body, 0)


def _norm_matmul_kernel(x_ref, g_ref, w_ref, o_ref, xn_ref, *, group_major):
    @pl.when(pl.program_id(1) == 0)
    def _():
        _rms_rows_to(x_ref, g_ref, xn_ref)

    res = jnp.dot(xn_ref[...], w_ref[...], preferred_element_type=F32)
    if group_major:
        for g in range(o_ref.shape[0]):
            o_ref[g] = res[:, g * LANES:(g + 1) * LANES].astype(o_ref.dtype)
    else:
        o_ref[...] = res.astype(o_ref.dtype)


def norm_matmul(x, g, w, *, layers, bm, bn, group_major=False, name):
    m, d = x.shape
    n = w.shape[2]
    first, nl = layers
    nj = n // bn
    if group_major:
        out_shape = jax.ShapeDtypeStruct((nl * n // LANES, m, LANES), BF16)
        out_spec = pl.BlockSpec((bn // LANES, bm, LANES), lambda i, j: (j, i, 0))
    else:
        out_shape = jax.ShapeDtypeStruct((m, nl * n), BF16)
        out_spec = pl.BlockSpec((bm, bn), lambda i, j: (i, j))
    return pl.pallas_call(
        functools.partial(_norm_matmul_kernel, group_major=group_major),
        grid=(m // bm, nl * nj),
        in_specs=[
            pl.BlockSpec((bm, d), lambda i, j: (i, 0)),
            pl.BlockSpec((1, d), lambda i, j: (0, 0)),
            pl.BlockSpec((None, d, bn), lambda i, j: (first + j // nj, 0, j % nj)),
        ],
        out_specs=out_spec,
        out_shape=out_shape,
        scratch_shapes=[pltpu.VMEM((bm, d), BF16)],
        compiler_params=_params(("parallel", "arbitrary")),
        name=name,
    )(x, g, w)


def _conv_module_kernel(av_ref, ag_ref, avh_ref, agh_ref, cw_ref, cb_ref, lg_ref, lb_ref,
                        o_ref, g_scr, c_scr, *, tiles_per_seq):
    ga, ts, _ = av_ref.shape
    halo = avh_ref.shape[1]
    ksize = cw_ref.shape[1]
    rows = _tile(ts, 128, SUBLANES)
    keep = jnp.where(pl.program_id(0) % tiles_per_seq == 0, 0.0, 1.0).astype(F32)

    def glu(val, gate):
        return val.astype(F32) * jax.nn.sigmoid(gate.astype(F32))

    def conv_group(g, carry):
        g_scr[g, :halo, :] = glu(avh_ref[g], agh_ref[g]) * keep
        g_scr[g, halo:, :] = glu(av_ref[g], ag_ref[g])
        for r0 in range(0, ts, rows):
            acc = jnp.zeros((rows, LANES), F32)
            for j in range(ksize):
                off = r0 + halo - (ksize - 1) + j
                acc = acc + cw_ref[g, j:j + 1, :] * g_scr[g, off:off + rows, :]
            c_scr[g, r0:r0 + rows, :] = acc + cb_ref[g]
        return carry

    lax.fori_loop(0, ga, conv_group, 0)

    width = ga * LANES
    nrows = _tile(ts, 64, SUBLANES)

    def norm_rows(c, carry):
        r = pl.ds(pl.multiple_of(c * nrows, nrows), nrows)
        tot = c_scr[0, r, :]
        for g in range(1, ga):
            tot = tot + c_scr[g, r, :]
        mu = jnp.sum(tot, axis=-1, keepdims=True) / width
        sq = jnp.square(c_scr[0, r, :] - mu)
        for g in range(1, ga):
            sq = sq + jnp.square(c_scr[g, r, :] - mu)
        inv = lax.rsqrt(jnp.sum(sq, axis=-1, keepdims=True) / width + EPS)
        for g in range(ga):
            y = (c_scr[g, r, :] - mu) * inv * lg_ref[g] + lb_ref[g]
            o_ref[r, g * LANES:(g + 1) * LANES] = (y * jax.nn.sigmoid(y)).astype(o_ref.dtype)
        return carry

    lax.fori_loop(0, ts // nrows, norm_rows, 0, unroll=2)


def conv_module(zg, conv_w, conv_b, ln_g, ln_b, *, seq, name):
    ksize, width = conv_w.shape
    ga = width // LANES
    t = zg.shape[1]
    ts = _tile(seq, 512, SUBLANES)
    halo = -(-(ksize - 1) // SUBLANES) * SUBLANES
    assert ts % halo == 0 and width % LANES == 0
    hb = ts // halo

    def per_group(v):
        return jnp.moveaxis(v.reshape(v.shape[:-1] + (ga, LANES)), -2, 0)

    cw = per_group(conv_w)
    cb, lg, lb = (per_group(v[None]) for v in (conv_b, ln_g, ln_b))
    full = lambda shape: pl.BlockSpec(shape, lambda i: (0,) * len(shape))
    return pl.pallas_call(
        functools.partial(_conv_module_kernel, tiles_per_seq=seq // ts),
        grid=(t // ts,),
        in_specs=[
            pl.BlockSpec((ga, ts, LANES), lambda i: (0, i, 0)),
            pl.BlockSpec((ga, ts, LANES), lambda i: (1, i, 0)),
            pl.BlockSpec((ga, halo, LANES), lambda i: (0, jnp.maximum(i * hb - 1, 0), 0)),
            pl.BlockSpec((ga, halo, LANES), lambda i: (1, jnp.maximum(i * hb - 1, 0), 0)),
            full((ga, ksize, LANES)), full((ga, 1, LANES)), full((ga, 1, LANES)), full((ga, 1, LANES)),
        ],
        out_specs=pl.BlockSpec((ts, width), lambda i: (i, 0)),
        out_shape=jax.ShapeDtypeStruct((t, width), BF16),
        scratch_shapes=[pltpu.VMEM((ga, halo + ts, LANES), F32), pltpu.VMEM((ga, ts, LANES), F32)],
        compiler_params=_params(("parallel",)),
        name=name,
    )(zg, zg, zg, zg, cw, cb, lg, lb)


def _diff_attn_kernel(qt_ref, kt_ref, q_ref, k_ref, v_ref, lq1_ref, lk1_ref, lq2_ref, lk2_ref, sg_ref,
                      o_ref, qs_scr, vx_scr, m_scr, acc_scr, *, lambda_init, chunk):
    step_id = pl.program_id(1)
    qi = qt_ref[step_id]
    kj = kt_ref[step_id]
    heads, tq, _ = q_ref.shape
    tk = k_ref.shape[1]
    hd = LANES // 2
    scale = hd ** -0.5

    @pl.when(kj == 0)
    def _init():
        lane = lax.broadcasted_iota(jnp.int32, (tq, LANES), 1)
        ones = jnp.ones((tk, LANES), BF16)

        def body(h, carry):
            q = (q_ref[h].astype(F32) * scale).astype(BF16)
            zero = jnp.zeros_like(q)
            qs_scr[h, :tq, :] = jnp.where(lane < hd, q, zero)
            qs_scr[h, tq:, :] = jnp.where(lane >= hd, q, zero)
            vx_scr[h, :, LANES:] = ones
            return carry

        lax.fori_loop(0, heads, body, 0)
        m_scr[...] = jnp.full(m_scr.shape, -jnp.inf, F32)
        acc_scr[...] = jnp.zeros(acc_scr.shape, F32)

    def step(masked):
        def body(h, carry):
            vx_scr[h, :, :LANES] = v_ref[h]
            for r0 in range(0, 2 * tq, chunk):
                rows = slice(r0, r0 + chunk)
                q0 = r0 % tq
                nk = min(tk, -(-(q0 + chunk) // chunk) * chunk) if masked else tk
                s = lax.dot_general(qs_scr[h, rows, :], k_ref[h, :nk, :], NT_DIMS,
                                    preferred_element_type=F32)
                if masked:
                    row = lax.broadcasted_iota(jnp.int32, (chunk, nk), 0) + q0
                    col = lax.broadcasted_iota(jnp.int32, (chunk, nk), 1)
                    s = jnp.where(col <= row, s, -jnp.inf)
                m_prev = m_scr[h, rows, :]
                m_new = jnp.maximum(m_prev, jnp.max(s, axis=-1, keepdims=True))
                alpha = jnp.exp(m_prev - m_new)
                p = jnp.exp(s - jnp.concatenate([m_new] * (nk // LANES), axis=1))
                pv = jnp.dot(p.astype(BF16), vx_scr[h, :nk, :], preferred_element_type=F32)
                acc_scr[h, rows, :] = jnp.concatenate([alpha, alpha], axis=1) * acc_scr[h, rows, :] + pv
                m_scr[h, rows, :] = m_new
            return carry

        lax.fori_loop(0, heads, body, 0, unroll=True)

    @pl.when(kj < qi)
    def _off_diagonal():
        step(False)

    @pl.when(kj == qi)
    def _diagonal():
        step(True)
        lam = (jnp.exp(jnp.sum(lq1_ref[...] * lk1_ref[...], axis=-1, keepdims=True))
               - jnp.exp(jnp.sum(lq2_ref[...] * lk2_ref[...], axis=-1, keepdims=True)) + lambda_init)

        for h in range(heads):
            o1 = acc_scr[h, :tq, :LANES] / acc_scr[h, :tq, LANES:]
            o2 = acc_scr[h, tq:, :LANES] / acc_scr[h, tq:, LANES:]
            o = o1 - lam * o2
            ms = jnp.mean(o * o, axis=-1, keepdims=True)
            y = o * lax.rsqrt(ms + EPS) * sg_ref[...]
            o_ref[:, h * LANES:(h + 1) * LANES] = (y * (1.0 - lambda_init)).astype(o_ref.dtype)


def diff_attention(zg, lq1, lk1, lq2, lk2, subln_g, *, batch, seq, a_width, heads, lambda_init, name):
    t = zg.shape[1]
    assert lq1.shape[-1] * 2 == LANES and subln_g.shape[-1] == LANES
    tq = _tile(seq, 512, SUBLANES)
    nq = seq // tq
    first_q = 2 * a_width // LANES
    assert first_q % heads == 0
    qb = first_q // heads
    pairs = [(i, j) for i in range(nq) for j in range(i + 1)]
    q_tab = jnp.asarray(np.array([p[0] for p in pairs], np.int32))
    k_tab = jnp.asarray(np.array([p[1] for p in pairs], np.int32))
    blk = (heads, tq, LANES)
    vec = lambda n: pl.BlockSpec((1, n), lambda b, s, qt, kt: (0, 0))
    grid_spec = pltpu.PrefetchScalarGridSpec(
        num_scalar_prefetch=2,
        grid=(batch, len(pairs)),
        in_specs=[
            pl.BlockSpec(blk, lambda b, s, qt, kt: (qb, b * nq + qt[s], 0)),
            pl.BlockSpec(blk, lambda b, s, qt, kt: (qb + 1, b * nq + kt[s], 0)),
            pl.BlockSpec(blk, lambda b, s, qt, kt: (qb + 2, b * nq + kt[s], 0)),
            vec(LANES // 2), vec(LANES // 2), vec(LANES // 2), vec(LANES // 2), vec(LANES),
        ],
        out_specs=pl.BlockSpec((tq, heads * LANES), lambda b, s, qt, kt: (b * nq + qt[s], 0)),
        scratch_shapes=[
            pltpu.VMEM((heads, 2 * tq, LANES), BF16),
            pltpu.VMEM((heads, tq, 2 * LANES), BF16),
            pltpu.VMEM((heads, 2 * tq, LANES), F32),
            pltpu.VMEM((heads, 2 * tq, 2 * LANES), F32),
        ],
    )
    return pl.pallas_call(
        functools.partial(_diff_attn_kernel, lambda_init=lambda_init, chunk=_tile(tq, 256, SUBLANES)),
        grid_spec=grid_spec,
        out_shape=jax.ShapeDtypeStruct((t, heads * LANES), BF16),
        compiler_params=_params(("parallel", "arbitrary")),
        name=name,
    )(q_tab, k_tab, zg, zg, zg, lq1[None], lk1[None], lq2[None], lk2[None], subln_g[None])


def _matmul_resid_kernel(*refs):
    *lhs_refs, w_ref, r_ref, out_ref = refs
    acc = r_ref[...]
    k0 = 0
    for lhs_ref in lhs_refs:
        k1 = k0 + lhs_ref.shape[1]
        acc = acc + jnp.dot(lhs_ref[...], w_ref[k0:k1, :], preferred_element_type=F32)
        k0 = k1
    out_ref[...] = acc


def matmul_resid(lhs, w, layer, resid, *, name):
    m = lhs[0].shape[0]
    _, k, n = w.shape
    assert sum(a.shape[1] for a in lhs) == k
    bm = _tile(m, 1024, SUBLANES)
    bn = _tile(n, 1024, LANES)
    return pl.pallas_call(
        _matmul_resid_kernel,
        grid=(m // bm, n // bn),
        in_specs=[pl.BlockSpec((bm, a.shape[1]), lambda i, j: (i, 0)) for a in lhs] + [
            pl.BlockSpec((None, k, bn), lambda i, j: (layer, 0, j)),
            pl.BlockSpec((bm, bn), lambda i, j: (i, j)),
        ],
        out_specs=pl.BlockSpec((bm, bn), lambda i, j: (i, j)),
        out_shape=jax.ShapeDtypeStruct((m, n), F32),
        compiler_params=_params(("parallel", "arbitrary")),
        name=name,
    )(*lhs, w, resid)


def _odd_mix_kernel(x_ref, xh_ref, g_ref, wb_ref, wc_ref, wu_ref, cw_ref, o_ref, xn_scr, cu_scr,
                    *, tiles_per_seq, chunk):
    bm = x_ref.shape[0]
    halo = xh_ref.shape[0]
    ksize = cw_ref.shape[0]

    @pl.when(pl.program_id(1) == 0)
    def _():
        _rms_rows_to(xh_ref, g_ref, xn_scr.at[pl.ds(0, halo)])
        _rms_rows_to(x_ref, g_ref, xn_scr.at[pl.ds(halo, bm)])

    keep = jnp.where(pl.program_id(0) % tiles_per_seq == 0, 0.0, 1.0).astype(F32)
    for r0 in range(0, bm, chunk):
        lo = 0 if r0 == 0 else halo + r0
        hi = halo + r0 + chunk
        xs = xn_scr[lo:hi, :]
        cu = (jnp.dot(xs, wc_ref[...], preferred_element_type=F32)
              * jnp.dot(xs, wu_ref[...], preferred_element_type=F32))
        if r0 == 0:
            cu_scr[:halo, :] = cu[:halo] * keep
            cu_scr[halo:hi, :] = cu[halo:]
        else:
            cu_scr[lo:hi, :] = cu
        gate = jnp.dot(xn_scr[halo + r0:hi, :], wb_ref[...], preferred_element_type=F32)
        acc = jnp.zeros_like(gate)
        for j in range(ksize):
            off = halo + r0 - (ksize - 1) + j
            acc = acc + cw_ref[j:j + 1, :] * cu_scr[off:off + chunk, :]
        o_ref[r0:r0 + chunk, :] = (gate * acc).astype(o_ref.dtype)


def odd_mix(x, g, w, layer, conv_w, *, seq, name):
    t, d = x.shape
    ksize, c = conv_w.shape
    assert w.shape[2] == 3 * c
    bm = _tile(seq, 1024, SUBLANES)
    bn = _tile(c, 512, LANES)
    nj = c // bn
    halo = 2 * SUBLANES
    assert ksize - 1 <= halo and bm % halo == 0
    hb = bm // halo
    wspec = lambda part: pl.BlockSpec((None, d, bn), lambda i, j: (layer, 0, part * nj + j))
    return pl.pallas_call(
        functools.partial(_odd_mix_kernel, tiles_per_seq=seq // bm, chunk=_tile(bm, 256, halo)),
        grid=(t // bm, nj),
        in_specs=[
            pl.BlockSpec((bm, d), lambda i, j: (i, 0)),
            pl.BlockSpec((halo, d), lambda i, j: (jnp.maximum(i * hb - 1, 0), 0)),
            pl.BlockSpec((1, d), lambda i, j: (0, 0)),
            wspec(0), wspec(1), wspec(2),
            pl.BlockSpec((ksize, bn), lambda i, j: (0, j)),
        ],
        out_specs=pl.BlockSpec((bm, bn), lambda i, j: (i, j)),
        out_shape=jax.ShapeDtypeStruct((t, c), BF16),
        scratch_shapes=[pltpu.VMEM((halo + bm, d), BF16), pltpu.VMEM((halo + bm, bn), F32)],
        compiler_params=_params(("parallel", "arbitrary")),
        name=name,
    )(x, x, g, w, w, w, conv_w)


def _xattn_mix_kernel(x_ref, g_ref, wq_ref, k_ref, v_ref, o_ref, xn_scr, *, chunk):
    @pl.when(pl.program_id(1) == 0)
    def _():
        _rms_rows_to(x_ref, g_ref, xn_scr)

    scale = wq_ref.shape[1] ** -0.5
    for r0 in range(0, x_ref.shape[0], chunk):
        rows = slice(r0, r0 + chunk)
        q = jnp.dot(xn_scr[rows, :], wq_ref[...], preferred_element_type=F32).astype(BF16)
        s = lax.dot_general(q, k_ref[...], NT_DIMS, preferred_element_type=F32) * scale
        e = jnp.exp(s - jnp.max(s, axis=-1, keepdims=True))
        p = e / jnp.sum(e, axis=-1, keepdims=True)
        o_ref[rows, :] = jnp.dot(p.astype(BF16), v_ref[...], preferred_element_type=F32).astype(o_ref.dtype)


def xattn_mix(x, g, wq, layer, k_all, v_all, *, seq, mem_len, name):
    t, d = x.shape
    hd = d // X_HEADS
    bm = _tile(seq, 1024, SUBLANES)
    tiles_per_seq = seq // bm
    kv_spec = pl.BlockSpec((mem_len, hd), lambda i, j: (i // tiles_per_seq, layer * X_HEADS + j))
    return pl.pallas_call(
        functools.partial(_xattn_mix_kernel, chunk=_tile(bm, 1024, SUBLANES)),
        grid=(t // bm, X_HEADS),
        in_specs=[
            pl.BlockSpec((bm, d), lambda i, j: (i, 0)),
            pl.BlockSpec((1, d), lambda i, j: (0, 0)),
            pl.BlockSpec((None, d, hd), lambda i, j: (layer, 0, j)),
            kv_spec, kv_spec,
        ],
        out_specs=pl.BlockSpec((bm, hd), lambda i, j: (i, j)),
        out_shape=jax.ShapeDtypeStruct((t, d), BF16),
        scratch_shapes=[pltpu.VMEM((bm, d), BF16)],
        compiler_params=_params(("parallel", "arbitrary")),
        name=name,
    )(x, g, wq, k_all, v_all)


def _mlp_kernel(x_ref, g_ref, w1_ref, w2_ref, fg_ref, o_ref, xn_ref, *, final_norm):
    j = pl.program_id(1)

    @pl.when(j == 0)
    def _():
        _rms_rows_to(x_ref, g_ref, xn_ref)
        o_ref[...] = x_ref[...]

    h = jnp.dot(xn_ref[...], w1_ref[...], preferred_element_type=F32)
    h = jnp.square(jnp.maximum(h, 0.0)).astype(BF16)
    o_ref[...] += jnp.dot(h, w2_ref[...], preferred_element_type=F32)

    if final_norm:
        @pl.when(j == pl.num_programs(1) - 1)
        def _():
            _rms_rows_to(o_ref, fg_ref, o_ref)


def mlp(x, g, w1, w2, layer, final_g, *, final_norm, name):
    m, d = x.shape
    f = w1.shape[2]
    bm = _tile(m, 512, SUBLANES)
    tf = _tile(f, 1024, LANES)
    return pl.pallas_call(
        functools.partial(_mlp_kernel, final_norm=final_norm),
        grid=(m // bm, f // tf),
        in_specs=[
            pl.BlockSpec((bm, d), lambda i, j: (i, 0)),
            pl.BlockSpec((1, d), lambda i, j: (0, 0)),
            pl.BlockSpec((None, d, tf), lambda i, j: (layer, 0, j)),
            pl.BlockSpec((None, tf, d), lambda i, j: (layer, j, 0)),
            pl.BlockSpec((1, d), lambda i, j: (0, 0)),
        ],
        out_specs=pl.BlockSpec((bm, d), lambda i, j: (i, 0)),
        out_shape=jax.ShapeDtypeStruct((m, d), F32),
        scratch_shapes=[pltpu.VMEM((bm, d), BF16)],
        compiler_params=_params(("parallel", "arbitrary")),
        name=name,
    )(x, g, w1, w2, final_g)


def kernel(x, mem, mem_norm_g, final_norm_g, mix_norm_g, xattn_norm_g, mlp_norm_g, even_w_in, conv_a_w, conv_a_b, ln_a_g, ln_a_b, lambda_q1, lambda_k1, lambda_q2, lambda_k2, subln_g, even_w_out, odd_w_in, conv_c_w, odd_w_out, xq_w, xk_w, xv_w, xo_w, mlp_w1, mlp_w2):
    batch, seq, d = x.shape
    mem_len = mem.shape[1]
    depth = xq_w.shape[0]
    a_width = conv_a_w.shape[-1]
    diff_heads = (d - a_width) // LANES
    t = batch * seq
    bf = lambda a: a.astype(BF16)
    row = lambda v: v[None].astype(F32)

    xf = x.reshape(t, d)
    bm = _tile(seq, 1024, SUBLANES)

    even_w_in, even_w_out, odd_w_in, odd_w_out, xq_w, xk_w, xv_w, xo_w, mlp_w1, mlp_w2 = (
        bf(w) for w in (even_w_in, even_w_out, odd_w_in, odd_w_out, xq_w, xk_w, xv_w, xo_w, mlp_w1, mlp_w2))
    bn_d = _tile(d, 1024, LANES)

    mem_f = mem.reshape(batch * mem_len, d)
    k_all = norm_matmul(mem_f, row(mem_norm_g), xk_w, layers=(0, depth), bm=batch * mem_len, bn=bn_d, name="mem_k")
    v_all = norm_matmul(mem_f, row(mem_norm_g), xv_w, layers=(0, depth), bm=batch * mem_len, bn=bn_d, name="mem_v")

    for l in range(depth):
        if l % 2 == 0:
            e = l // 2
            lambda_init = 0.8 - 0.6 * float(np.exp(-0.3 * l))
            zg = norm_matmul(xf, row(mix_norm_g[l]), even_w_in, layers=(e, 1), bm=bm,
                             bn=_tile(even_w_in.shape[-1], 1024, LANES), group_major=True, name=f"even_in_{l}")
            a = conv_module(zg, conv_a_w[e], conv_a_b[e], ln_a_g[e], ln_a_b[e], seq=seq, name=f"conv_module_{l}")
            o = diff_attention(zg, lambda_q1[e], lambda_k1[e], lambda_q2[e], lambda_k2[e], subln_g[e],
                               batch=batch, seq=seq, a_width=a_width, heads=diff_heads,
                               lambda_init=lambda_init, name=f"diff_attn_{l}")
            xf = matmul_resid([a, o], even_w_out, e, xf, name=f"even_out_{l}")
        else:
            o = l // 2
            y = odd_mix(xf, row(mix_norm_g[l]), odd_w_in, o, conv_c_w[o], seq=seq, name=f"odd_mix_{l}")
            xf = matmul_resid([y], odd_w_out, o, xf, name=f"odd_out_{l}")
        att = xattn_mix(xf, row(xattn_norm_g[l]), xq_w, l, k_all, v_all, seq=seq, mem_len=mem_len,
                        name=f"xattn_mix_{l}")
        xf = matmul_resid([att], xo_w, l, xf, name=f"xattn_out_{l}")
        xf = mlp(xf, row(mlp_norm_g[l]), mlp_w1, mlp_w2, l, row(final_norm_g),
                 final_norm=(l == depth - 1), name=f"mlp_{l}")
    return xf.reshape(batch, seq, d)
```

```python
import functools

import numpy as np
import jax
import jax.numpy as jnp
from jax import lax
from jax.experimental import pallas as pl
from jax.experimental.pallas import tpu as pltpu

F32 = jnp.float32
BF16 = jnp.bfloat16
EPS = 1e-6
X_HEADS = 4
LANES = 128
SUBLANES = 8
VMEM_LIMIT_BYTES = 56 * 1024 * 1024
NT_DIMS = (((1,), (1,)), ((), ()))


def _params(semantics):
    return pltpu.CompilerParams(dimension_semantics=semantics, vmem_limit_bytes=VMEM_LIMIT_BYTES)


def _tile(n, target, quantum):
    t = min(n, target) // quantum * quantum
    while t > quantum and n % t:
        t -= quantum
    assert t > 0 and n % t == 0, (n, target, quantum)
    return t


def _rms_rows_to(x_ref, g_ref, dst_ref):
    n = x_ref.shape[0]
    rows = _tile(n, 256, SUBLANES)
    g = g_ref[...]

    def body(c, carry):
        r = pl.ds(pl.multiple_of(c * rows, rows), rows)
        x = x_ref[r, :].astype(F32)
        ms = jnp.mean(x * x, axis=-1, keepdims=True)
        dst_ref[r, :] = (x * lax.rsqrt(ms + EPS) * g).astype(dst_ref.dtype)
        return carry

    lax.fori_loop(0, n // rows, body, 0)


def _norm_matmul_kernel(x_ref, g_ref, w_ref, o_ref, xn_ref, *, group_major):
    @pl.when(pl.program_id(1) == 0)
    def _():
        _rms_rows_to(x_ref, g_ref, xn_ref)

    res = jnp.dot(xn_ref[...], w_ref[...], preferred_element_type=F32)
    if group_major:
        for g in range(o_ref.shape[0]):
            o_ref[g] = res[:, g * LANES:(g + 1) * LANES].astype(o_ref.dtype)
    else:
        o_ref[...] = res.astype(o_ref.dtype)


def norm_matmul(x, g, w, *, layers, bm, bn, group_major=False, name):
    m, d = x.shape
    n = w.shape[2]
    first, nl = layers
    nj = n // bn
    if group_major:
        out_shape = jax.ShapeDtypeStruct((nl * n // LANES, m, LANES), BF16)
        out_spec = pl.BlockSpec((bn // LANES, bm, LANES), lambda i, j: (j, i, 0))
    else:
        out_shape = jax.ShapeDtypeStruct((m, nl * n), BF16)
        out_spec = pl.BlockSpec((bm, bn), lambda i, j: (i, j))
    return pl.pallas_call(
        functools.partial(_norm_matmul_kernel, group_major=group_major),
        grid=(m // bm, nl * nj),
        in_specs=[
            pl.BlockSpec((bm, d), lambda i, j: (i, 0)),
            pl.BlockSpec((1, d), lambda i, j: (0, 0)),
            pl.BlockSpec((None, d, bn), lambda i, j: (first + j // nj, 0, j % nj)),
        ],
        out_specs=out_spec,
        out_shape=out_shape,
        scratch_shapes=[pltpu.VMEM((bm, d), BF16)],
        compiler_params=_params(("parallel", "arbitrary")),
        name=name,
    )(x, g, w)


def _conv_module_kernel(av_ref, ag_ref, avh_ref, agh_ref, cw_ref, cb_ref, lg_ref, lb_ref,
                        o_ref, g_scr, c_scr, *, tiles_per_seq):
    ga, ts, _ = av_ref.shape
    halo = avh_ref.shape[1]
    ksize = cw_ref.shape[1]
    rows = _tile(ts, 128, SUBLANES)
    keep = jnp.where(pl.program_id(0) % tiles_per_seq == 0, 0.0, 1.0).astype(F32)

    def glu(val, gate):
        return val.astype(F32) * jax.nn.sigmoid(gate.astype(F32))

    def conv_group(g, carry):
        g_scr[g, :halo, :] = glu(avh_ref[g], agh_ref[g]) * keep
        g_scr[g, halo:, :] = glu(av_ref[g], ag_ref[g])
        for r0 in range(0, ts, rows):
            acc = jnp.zeros((rows, LANES), F32)
            for j in range(ksize):
                off = r0 + halo - (ksize - 1) + j
                acc = acc + cw_ref[g, j:j + 1, :] * g_scr[g, off:off + rows, :]
            c_scr[g, r0:r0 + rows, :] = acc + cb_ref[g]
        return carry

    lax.fori_loop(0, ga, conv_group, 0)

    width = ga * LANES
    nrows = _tile(ts, 64, SUBLANES)

    def norm_rows(c, carry):
        r = pl.ds(pl.multiple_of(c * nrows, nrows), nrows)
        tot = c_scr[0, r, :]
        for g in range(1, ga):
            tot = tot + c_scr[g, r, :]
        mu = jnp.sum(tot, axis=-1, keepdims=True) / width
        sq = jnp.square(c_scr[0, r, :] - mu)
        for g in range(1, ga):
            sq = sq + jnp.square(c_scr[g, r, :] - mu)
        inv = lax.rsqrt(jnp.sum(sq, axis=-1, keepdims=True) / width + EPS)
        for g in range(ga):
            y = (c_scr[g, r, :] - mu) * inv * lg_ref[g] + lb_ref[g]
            o_ref[r, g * LANES:(g + 1) * LANES] = (y * jax.nn.sigmoid(y)).astype(o_ref.dtype)
        return carry

    lax.fori_loop(0, ts // nrows, norm_rows, 0, unroll=2)


def conv_module(zg, conv_w, conv_b, ln_g, ln_b, *, seq, name):
    ksize, width = conv_w.shape
    ga = width // LANES
    t = zg.shape[1]
    ts = _tile(seq, 512, SUBLANES)
    halo = -(-(ksize - 1) // SUBLANES) * SUBLANES
    assert ts % halo == 0 and width % LANES == 0
    hb = ts // halo

    def per_group(v):
        return jnp.moveaxis(v.reshape(v.shape[:-1] + (ga, LANES)), -2, 0)

    cw = per_group(conv_w)
    cb, lg, lb = (per_group(v[None]) for v in (conv_b, ln_g, ln_b))
    full = lambda shape: pl.BlockSpec(shape, lambda i: (0,) * len(shape))
    return pl.pallas_call(
        functools.partial(_conv_module_kernel, tiles_per_seq=seq // ts),
        grid=(t // ts,),
        in_specs=[
            pl.BlockSpec((ga, ts, LANES), lambda i: (0, i, 0)),
            pl.BlockSpec((ga, ts, LANES), lambda i: (1, i, 0)),
            pl.BlockSpec((ga, halo, LANES), lambda i: (0, jnp.maximum(i * hb - 1, 0), 0)),
            pl.BlockSpec((ga, halo, LANES), lambda i: (1, jnp.maximum(i * hb - 1, 0), 0)),
            full((ga, ksize, LANES)), full((ga, 1, LANES)), full((ga, 1, LANES)), full((ga, 1, LANES)),
        ],
        out_specs=pl.BlockSpec((ts, width), lambda i: (i, 0)),
        out_shape=jax.ShapeDtypeStruct((t, width), BF16),
        scratch_shapes=[pltpu.VMEM((ga, halo + ts, LANES), F32), pltpu.VMEM((ga, ts, LANES), F32)],
        compiler_params=_params(("parallel",)),
        name=name,
    )(zg, zg, zg, zg, cw, cb, lg, lb)


def _diff_attn_kernel(qt_ref, kt_ref, q_ref, k_ref, v_ref, lq1_ref, lk1_ref, lq2_ref, lk2_ref, sg_ref,
                      o_ref, qs_scr, vx_scr, m_scr, acc_scr, *, lambda_init, chunk):
    step_id = pl.program_id(1)
    qi = qt_ref[step_id]
    kj = kt_ref[step_id]
    heads, tq, _ = q_ref.shape
    tk = k_ref.shape[1]
    hd = LANES // 2
    scale = hd ** -0.5

    @pl.when(kj == 0)
    def _init():
        lane = lax.broadcasted_iota(jnp.int32, (tq, LANES), 1)
        ones = jnp.ones((tk, LANES), BF16)

        def body(h, carry):
            q = (q_ref[h].astype(F32) * scale).astype(BF16)
            zero = jnp.zeros_like(q)
            qs_scr[h, :tq, :] = jnp.where(lane < hd, q, zero)
            qs_scr[h, tq:, :] = jnp.where(lane >= hd, q, zero)
            vx_scr[h, :, LANES:] = ones
            return carry

        lax.fori_loop(0, heads, body, 0)
        m_scr[...] = jnp.full(m_scr.shape, -jnp.inf, F32)
        acc_scr[...] = jnp.zeros(acc_scr.shape, F32)

    def step(masked):
        def body(h, carry):
            vx_scr[h, :, :LANES] = v_ref[h]
            for r0 in range(0, 2 * tq, chunk):
                rows = slice(r0, r0 + chunk)
                q0 = r0 % tq
                nk = min(tk, -(-(q0 + chunk) // chunk) * chunk) if masked else tk
                s = lax.dot_general(qs_scr[h, rows, :], k_ref[h, :nk, :], NT_DIMS,
                                    preferred_element_type=F32)
                if masked:
                    row = lax.broadcasted_iota(jnp.int32, (chunk, nk), 0) + q0
                    col = lax.broadcasted_iota(jnp.int32, (chunk, nk), 1)
                    s = jnp.where(col <= row, s, -jnp.inf)
                m_prev = m_scr[h, rows, :]
                m_new = jnp.maximum(m_prev, jnp.max(s, axis=-1, keepdims=True))
                alpha = jnp.exp(m_prev - m_new)
                p = jnp.exp(s - jnp.concatenate([m_new] * (nk // LANES), axis=1))
                pv = jnp.dot(p.astype(BF16), vx_scr[h, :nk, :], preferred_element_type=F32)
                acc_scr[h, rows, :] = jnp.concatenate([alpha, alpha], axis=1) * acc_scr[h, rows, :] + pv
                m_scr[h, rows, :] = m_new
            return carry

        lax.fori_loop(0, heads, body, 0, unroll=True)

    @pl.when(kj < qi)
    def _off_diagonal():
        step(False)

    @pl.when(kj == qi)
    def _diagonal():
        step(True)
        lam = (jnp.exp(jnp.sum(lq1_ref[...] * lk1_ref[...], axis=-1, keepdims=True))
               - jnp.exp(jnp.sum(lq2_ref[...] * lk2_ref[...], axis=-1, keepdims=True)) + lambda_init)

        for h in range(heads):
            o1 = acc_scr[h, :tq, :LANES] / acc_scr[h, :tq, LANES:]
            o2 = acc_scr[h, tq:, :LANES] / acc_scr[h, tq:, LANES:]
            o = o1 - lam * o2
            ms = jnp.mean(o * o, axis=-1, keepdims=True)
            y = o * lax.rsqrt(ms + EPS) * sg_ref[...]
            o_ref[:, h * LANES:(h + 1) * LANES] = (y * (1.0 - lambda_init)).astype(o_ref.dtype)


def diff_attention(zg, lq1, lk1, lq2, lk2, subln_g, *, batch, seq, a_width, heads, lambda_init, name):
    t = zg.shape[1]
    assert lq1.shape[-1] * 2 == LANES and subln_g.shape[-1] == LANES
    tq = _tile(seq, 512, SUBLANES)
    nq = seq // tq
    first_q = 2 * a_width // LANES
    assert first_q % heads == 0
    qb = first_q // heads
    pairs = [(i, j) for i in range(nq) for j in range(i + 1)]
    q_tab = jnp.asarray(np.array([p[0] for p in pairs], np.int32))
    k_tab = jnp.asarray(np.array([p[1] for p in pairs], np.int32))
    blk = (heads, tq, LANES)
    vec = lambda n: pl.BlockSpec((1, n), lambda b, s, qt, kt: (0, 0))
    grid_spec = pltpu.PrefetchScalarGridSpec(
        num_scalar_prefetch=2,
        grid=(batch, len(pairs)),
        in_specs=[
            pl.BlockSpec(blk, lambda b, s, qt, kt: (qb, b * nq + qt[s], 0)),
            pl.BlockSpec(blk, lambda b, s, qt, kt: (qb + 1, b * nq + kt[s], 0)),
            pl.BlockSpec(blk, lambda b, s, qt, kt: (qb + 2, b * nq + kt[s], 0)),
            vec(LANES // 2), vec(LANES // 2), vec(LANES // 2), vec(LANES // 2), vec(LANES),
        ],
        out_specs=pl.BlockSpec((tq, heads * LANES), lambda b, s, qt, kt: (b * nq + qt[s], 0)),
        scratch_shapes=[
            pltpu.VMEM((heads, 2 * tq, LANES), BF16),
            pltpu.VMEM((heads, tq, 2 * LANES), BF16),
            pltpu.VMEM((heads, 2 * tq, LANES), F32),
            pltpu.VMEM((heads, 2 * tq, 2 * LANES), F32),
        ],
    )
    return pl.pallas_call(
        functools.partial(_diff_attn_kernel, lambda_init=lambda_init, chunk=_tile(tq, 256, SUBLANES)),
        grid_spec=grid_spec,
        out_shape=jax.ShapeDtypeStruct((t, heads * LANES), BF16),
        compiler_params=_params(("parallel", "arbitrary")),
        name=name,
    )(q_tab, k_tab, zg, zg, zg, lq1[None], lk1[None], lq2[None], lk2[None], subln_g[None])


def _matmul_resid_kernel(*refs):
    *lhs_refs, w_ref, r_ref, out_ref = refs
    acc = r_ref[...]
    k0 = 0
    for lhs_ref in lhs_refs:
        k1 = k0 + lhs_ref.shape[1]
        acc = acc + jnp.dot(lhs_ref[...], w_ref[k0:k1, :], preferred_element_type=F32)
        k0 = k1
    out_ref[...] = acc


def matmul_resid(lhs, w, layer, resid, *, name):
    m = lhs[0].shape[0]
    _, k, n = w.shape
    assert sum(a.shape[1] for a in lhs) == k
    bm = _tile(m, 512, SUBLANES)
    bn = _tile(n, 2048, LANES)
    return pl.pallas_call(
        _matmul_resid_kernel,
        grid=(m // bm, n // bn),
        in_specs=[pl.BlockSpec((bm, a.shape[1]), lambda i, j: (i, 0)) for a in lhs] + [
            pl.BlockSpec((None, k, bn), lambda i, j: (layer, 0, j)),
            pl.BlockSpec((bm, bn), lambda i, j: (i, j)),
        ],
        out_specs=pl.BlockSpec((bm, bn), lambda i, j: (i, j)),
        out_shape=jax.ShapeDtypeStruct((m, n), F32),
        compiler_params=_params(("parallel", "arbitrary")),
        name=name,
    )(*lhs, w, resid)


def _odd_mix_kernel(x_ref, xh_ref, g_ref, wb_ref, wc_ref, wu_ref, cw_ref, o_ref, xn_scr, cu_scr,
                    *, tiles_per_seq, chunk):
    bm = x_ref.shape[0]
    halo = xh_ref.shape[0]
    ksize = cw_ref.shape[0]

    @pl.when(pl.program_id(1) == 0)
    def _():
        _rms_rows_to(xh_ref, g_ref, xn_scr.at[pl.ds(0, halo)])
        _rms_rows_to(x_ref, g_ref, xn_scr.at[pl.ds(halo, bm)])

    keep = jnp.where(pl.program_id(0) % tiles_per_seq == 0, 0.0, 1.0).astype(F32)
    for r0 in range(0, bm, chunk):
        lo = 0 if r0 == 0 else halo + r0
        hi = halo + r0 + chunk
        xs = xn_scr[lo:hi, :]
        cu = (jnp.dot(xs, wc_ref[...], preferred_element_type=F32)
              * jnp.dot(xs, wu_ref[...], preferred_element_type=F32))
        if r0 == 0:
            cu_scr[:halo, :] = cu[:halo] * keep
            cu_scr[halo:hi, :] = cu[halo:]
        else:
            cu_scr[lo:hi, :] = cu
        gate = jnp.dot(xn_scr[halo + r0:hi, :], wb_ref[...], preferred_element_type=F32)
        acc = jnp.zeros_like(gate)
        for j in range(ksize):
            off = halo + r0 - (ksize - 1) + j
            acc = acc + cw_ref[j:j + 1, :] * cu_scr[off:off + chunk, :]
        o_ref[r0:r0 + chunk, :] = (gate * acc).astype(o_ref.dtype)


def odd_mix(x, g, w, layer, conv_w, *, seq, name):
    t, d = x.shape
    ksize, c = conv_w.shape
    assert w.shape[2] == 3 * c
    bm = _tile(seq, 1024, SUBLANES)
    bn = _tile(c, 512, LANES)
    nj = c // bn
    halo = 2 * SUBLANES
    assert ksize - 1 <= halo and bm % halo == 0
    hb = bm // halo
    wspec = lambda part: pl.BlockSpec((None, d, bn), lambda i, j: (layer, 0, part * nj + j))
    return pl.pallas_call(
        functools.partial(_odd_mix_kernel, tiles_per_seq=seq // bm, chunk=_tile(bm, 256, halo)),
        grid=(t // bm, nj),
        in_specs=[
            pl.BlockSpec((bm, d), lambda i, j: (i, 0)),
            pl.BlockSpec((halo, d), lambda i, j: (jnp.maximum(i * hb - 1, 0), 0)),
            pl.BlockSpec((1, d), lambda i, j: (0, 0)),
            wspec(0), wspec(1), wspec(2),
            pl.BlockSpec((ksize, bn), lambda i, j: (0, j)),
        ],
        out_specs=pl.BlockSpec((bm, bn), lambda i, j: (i, j)),
        out_shape=jax.ShapeDtypeStruct((t, c), BF16),
        scratch_shapes=[pltpu.VMEM((halo + bm, d), BF16), pltpu.VMEM((halo + bm, bn), F32)],
        compiler_params=_params(("parallel", "arbitrary")),
        name=name,
    )(x, x, g, w, w, w, conv_w)


def _xattn_mix_kernel(x_ref, g_ref, wq_ref, k_ref, v_ref, o_ref, xn_scr, *, hd):
    @pl.when(pl.program_id(1) == 0)
    def _():
        _rms_rows_to(x_ref, g_ref, xn_scr)

    scale = hd ** -0.5
    for c0 in range(0, wq_ref.shape[1], hd):
        cols = slice(c0, c0 + hd)
        q = jnp.dot(xn_scr[...], wq_ref[:, cols], preferred_element_type=F32).astype(BF16)
        s = lax.dot_general(q, k_ref[:, cols], NT_DIMS, preferred_element_type=F32) * scale
        e = jnp.exp(s - jnp.max(s, axis=-1, keepdims=True))
        p = e / jnp.sum(e, axis=-1, keepdims=True)
        o_ref[:, cols] = jnp.dot(p.astype(BF16), v_ref[:, cols], preferred_element_type=F32).astype(o_ref.dtype)


def xattn_mix(x, g, wq, layer, k_all, v_all, *, seq, mem_len, name):
    t, d = x.shape
    hd = d // X_HEADS
    bm = _tile(seq, 1024, SUBLANES)
    tiles_per_seq = seq // bm
    bn = 2 * hd
    nj = d // bn
    kv_spec = pl.BlockSpec((mem_len, bn), lambda i, j: (i // tiles_per_seq, layer * nj + j))
    return pl.pallas_call(
        functools.partial(_xattn_mix_kernel, hd=hd),
        grid=(t // bm, nj),
        in_specs=[
            pl.BlockSpec((bm, d), lambda i, j: (i, 0)),
            pl.BlockSpec((1, d), lambda i, j: (0, 0)),
            pl.BlockSpec((None, d, bn), lambda i, j: (layer, 0, j)),
            kv_spec, kv_spec,
        ],
        out_specs=pl.BlockSpec((bm, bn), lambda i, j: (i, j)),
        out_shape=jax.ShapeDtypeStruct((t, d), BF16),
        scratch_shapes=[pltpu.VMEM((bm, d), BF16)],
        compiler_params=_params(("parallel", "arbitrary")),
        name=name,
    )(x, g, wq, k_all, v_all)


def _mlp_kernel(x_ref, att_ref, wo_ref, g_ref, w1_ref, w2_ref, fg_ref, o_ref, xn_ref, *, final_norm):
    j = pl.program_id(1)

    @pl.when(j == 0)
    def _():
        o_ref[...] = x_ref[...] + jnp.dot(att_ref[...], wo_ref[...], preferred_element_type=F32)
        _rms_rows_to(o_ref, g_ref, xn_ref)

    h = jnp.dot(xn_ref[...], w1_ref[...], preferred_element_type=F32)
    h = jnp.square(jnp.maximum(h, 0.0)).astype(BF16)
    o_ref[...] += jnp.dot(h, w2_ref[...], preferred_element_type=F32)

    if final_norm:
        @pl.when(j == pl.num_programs(1) - 1)
        def _():
            _rms_rows_to(o_ref, fg_ref, o_ref)


def mlp(x, att, wo, g, w1, w2, layer, final_g, *, final_norm, name):
    m, d = x.shape
    f = w1.shape[2]
    bm = _tile(m, 512, SUBLANES)
    tf = _tile(f, 1024, LANES)
    return pl.pallas_call(
        functools.partial(_mlp_kernel, final_norm=final_norm),
        grid=(m // bm, f // tf),
        in_specs=[
            pl.BlockSpec((bm, d), lambda i, j: (i, 0)),
            pl.BlockSpec((bm, d), lambda i, j: (i, 0)),
            pl.BlockSpec((None, d, d), lambda i, j: (layer, 0, 0), pipeline_mode=pl.Buffered(1)),
            pl.BlockSpec((1, d), lambda i, j: (0, 0)),
            pl.BlockSpec((None, d, tf), lambda i, j: (layer, 0, j)),
            pl.BlockSpec((None, tf, d), lambda i, j: (layer, j, 0)),
            pl.BlockSpec((1, d), lambda i, j: (0, 0)),
        ],
        out_specs=pl.BlockSpec((bm, d), lambda i, j: (i, 0)),
        out_shape=jax.ShapeDtypeStruct((m, d), F32),
        scratch_shapes=[pltpu.VMEM((bm, d), BF16)],
        compiler_params=_params(("parallel", "arbitrary")),
        name=name,
    )(x, att, wo, g, w1, w2, final_g)


def kernel(x, mem, mem_norm_g, final_norm_g, mix_norm_g, xattn_norm_g, mlp_norm_g, even_w_in, conv_a_w, conv_a_b, ln_a_g, ln_a_b, lambda_q1, lambda_k1, lambda_q2, lambda_k2, subln_g, even_w_out, odd_w_in, conv_c_w, odd_w_out, xq_w, xk_w, xv_w, xo_w, mlp_w1, mlp_w2):
    batch, seq, d = x.shape
    mem_len = mem.shape[1]
    depth = xq_w.shape[0]
    a_width = conv_a_w.shape[-1]
    diff_heads = (d - a_width) // LANES
    t = batch * seq
    bf = lambda a: a.astype(BF16)
    row = lambda v: v[None].astype(F32)

    xf = x.reshape(t, d)
    bm = _tile(seq, 1024, SUBLANES)

    even_w_in, even_w_out, odd_w_in, odd_w_out, xq_w, xk_w, xv_w, xo_w, mlp_w1, mlp_w2 = (
        bf(w) for w in (even_w_in, even_w_out, odd_w_in, odd_w_out, xq_w, xk_w, xv_w, xo_w, mlp_w1, mlp_w2))
    bn_d = _tile(d, 1024, LANES)

    mem_f = mem.reshape(batch * mem_len, d)
    k_all = norm_matmul(mem_f, row(mem_norm_g), xk_w, layers=(0, depth), bm=batch * mem_len, bn=bn_d, name="mem_k")
    v_all = norm_matmul(mem_f, row(mem_norm_g), xv_w, layers=(0, depth), bm=batch * mem_len, bn=bn_d, name="mem_v")

    for l in range(depth):
        if l % 2 == 0:
            e = l // 2
            lambda_init = 0.8 - 0.6 * float(np.exp(-0.3 * l))
            zg = norm_matmul(xf, row(mix_norm_g[l]), even_w_in, layers=(e, 1), bm=bm,
                             bn=_tile(even_w_in.shape[-1], 1024, LANES), group_major=True, name=f"even_in_{l}")
            a = conv_module(zg, conv_a_w[e], conv_a_b[e], ln_a_g[e], ln_a_b[e], seq=seq, name=f"conv_module_{l}")
            o = diff_attention(zg, lambda_q1[e], lambda_k1[e], lambda_q2[e], lambda_k2[e], subln_g[e],
                               batch=batch, seq=seq, a_width=a_width, heads=diff_heads,
                               lambda_init=lambda_init, name=f"diff_attn_{l}")
            xf = matmul_resid([a, o], even_w_out, e, xf, name=f"even_out_{l}")
        else:
            o = l // 2
            y = odd_mix(xf, row(mix_norm_g[l]), odd_w_in, o, conv_c_w[o], seq=seq, name=f"odd_mix_{l}")
            xf = matmul_resid([y], odd_w_out, o, xf, name=f"odd_out_{l}")
        att = xattn_mix(xf, row(xattn_norm_g[l]), xq_w, l, k_all, v_all, seq=seq, mem_len=mem_len,
                        name=f"xattn_mix_{l}")
        xf = mlp(xf, att, xo_w, row(mlp_norm_g[l]), mlp_w1, mlp_w2, l, row(final_norm_g),
                 final_norm=(l == depth - 1), name=f"mlp_{l}")
    return xf.reshape(batch, seq, d)
```

```python
import functools

import numpy as np
import jax
import jax.numpy as jnp
from jax import lax
from jax.experimental import pallas as pl
from jax.experimental.pallas import tpu as pltpu

F32 = jnp.float32
BF16 = jnp.bfloat16
EPS = 1e-6
X_HEADS = 4
LANES = 128
SUBLANES = 8
VMEM_LIMIT_BYTES = 56 * 1024 * 1024
NT_DIMS = (((1,), (1,)), ((), ()))


def _params(semantics):
    return pltpu.CompilerParams(dimension_semantics=semantics, vmem_limit_bytes=VMEM_LIMIT_BYTES)


def _tile(n, target, quantum):
    t = min(n, target) // quantum * quantum
    while t > quantum and n % t:
        t -= quantum
    assert t > 0 and n % t == 0, (n, target, quantum)
    return t


def _rms_rows_to(x_ref, g_ref, dst_ref):
    n = x_ref.shape[0]
    rows = _tile(n, 256, SUBLANES)
    g = g_ref[...]

    def body(c, carry):
        r = pl.ds(pl.multiple_of(c * rows, rows), rows)
        x = x_ref[r, :].astype(F32)
        ms = jnp.mean(x * x, axis=-1, keepdims=True)
        dst_ref[r, :] = (x * lax.rsqrt(ms + EPS) * g).astype(dst_ref.dtype)
        return carry

    lax.fori_loop(0, n // rows, body, 0)


def _norm_matmul_kernel(x_ref, g_ref, w_ref, o_ref, xn_ref, *, group_major):
    @pl.when(pl.program_id(1) == 0)
    def _():
        _rms_rows_to(x_ref, g_ref, xn_ref)

    res = jnp.dot(xn_ref[...], w_ref[...], preferred_element_type=F32)
    if group_major:
        for g in range(o_ref.shape[0]):
            o_ref[g] = res[:, g * LANES:(g + 1) * LANES].astype(o_ref.dtype)
    else:
        o_ref[...] = res.astype(o_ref.dtype)


def norm_matmul(x, g, w, *, layers, bm, bn, group_major=False, name):
    m, d = x.shape
    n = w.shape[2]
    first, nl = layers
    nj = n // bn
    if group_major:
        out_shape = jax.ShapeDtypeStruct((nl * n // LANES, m, LANES), BF16)
        out_spec = pl.BlockSpec((bn // LANES, bm, LANES), lambda i, j: (j, i, 0))
    else:
        out_shape = jax.ShapeDtypeStruct((m, nl * n), BF16)
        out_spec = pl.BlockSpec((bm, bn), lambda i, j: (i, j))
    return pl.pallas_call(
        functools.partial(_norm_matmul_kernel, group_major=group_major),
        grid=(m // bm, nl * nj),
        in_specs=[
            pl.BlockSpec((bm, d), lambda i, j: (i, 0)),
            pl.BlockSpec((1, d), lambda i, j: (0, 0)),
            pl.BlockSpec((None, d, bn), lambda i, j: (first + j // nj, 0, j % nj)),
        ],
        out_specs=out_spec,
        out_shape=out_shape,
        scratch_shapes=[pltpu.VMEM((bm, d), BF16)],
        compiler_params=_params(("parallel", "arbitrary")),
        name=name,
    )(x, g, w)


def _conv_module_kernel(av_ref, ag_ref, avh_ref, agh_ref, cw_ref, cb_ref, lg_ref, lb_ref,
                        o_ref, g_scr, c_scr, *, tiles_per_seq):
    ga, ts, _ = av_ref.shape
    halo = avh_ref.shape[1]
    ksize = cw_ref.shape[1]
    rows = _tile(ts, 128, SUBLANES)
    keep = jnp.where(pl.program_id(0) % tiles_per_seq == 0, 0.0, 1.0).astype(F32)

    def glu(val, gate):
        return val.astype(F32) * jax.nn.sigmoid(gate.astype(F32))

    def conv_group(g, carry):
        g_scr[g, :halo, :] = glu(avh_ref[g], agh_ref[g]) * keep
        g_scr[g, halo:, :] = glu(av_ref[g], ag_ref[g])
        for r0 in range(0, ts, rows):
            acc = jnp.zeros((rows, LANES), F32)
            for j in range(ksize):
                off = r0 + halo - (ksize - 1) + j
                acc = acc + cw_ref[g, j:j + 1, :] * g_scr[g, off:off + rows, :]
            c_scr[g, r0:r0 + rows, :] = acc + cb_ref[g]
        return carry

    lax.fori_loop(0, ga, conv_group, 0)

    width = ga * LANES
    nrows = _tile(ts, 64, SUBLANES)

    def norm_rows(c, carry):
        r = pl.ds(pl.multiple_of(c * nrows, nrows), nrows)
        tot = c_scr[0, r, :]
        for g in range(1, ga):
            tot = tot + c_scr[g, r, :]
        mu = jnp.sum(tot, axis=-1, keepdims=True) / width
        sq = jnp.square(c_scr[0, r, :] - mu)
        for g in range(1, ga):
            sq = sq + jnp.square(c_scr[g, r, :] - mu)
        inv = lax.rsqrt(jnp.sum(sq, axis=-1, keepdims=True) / width + EPS)
        for g in range(ga):
            y = (c_scr[g, r, :] - mu) * inv * lg_ref[g] + lb_ref[g]
            o_ref[r, g * LANES:(g + 1) * LANES] = (y * jax.nn.sigmoid(y)).astype(o_ref.dtype)
        return carry

    lax.fori_loop(0, ts // nrows, norm_rows, 0, unroll=2)


def conv_module(zg, conv_w, conv_b, ln_g, ln_b, *, seq, name):
    ksize, width = conv_w.shape
    ga = width // LANES
    t = zg.shape[1]
    ts = _tile(seq, 512, SUBLANES)
    halo = -(-(ksize - 1) // SUBLANES) * SUBLANES
    assert ts % halo == 0 and width % LANES == 0
    hb = ts // halo

    def per_group(v):
        return jnp.moveaxis(v.reshape(v.shape[:-1] + (ga, LANES)), -2, 0)

    cw = per_group(conv_w)
    cb, lg, lb = (per_group(v[None]) for v in (conv_b, ln_g, ln_b))
    full = lambda shape: pl.BlockSpec(shape, lambda i: (0,) * len(shape))
    return pl.pallas_call(
        functools.partial(_conv_module_kernel, tiles_per_seq=seq // ts),
        grid=(t // ts,),
        in_specs=[
            pl.BlockSpec((ga, ts, LANES), lambda i: (0, i, 0)),
            pl.BlockSpec((ga, ts, LANES), lambda i: (1, i, 0)),
            pl.BlockSpec((ga, halo, LANES), lambda i: (0, jnp.maximum(i * hb - 1, 0), 0)),
            pl.BlockSpec((ga, halo, LANES), lambda i: (1, jnp.maximum(i * hb - 1, 0), 0)),
            full((ga, ksize, LANES)), full((ga, 1, LANES)), full((ga, 1, LANES)), full((ga, 1, LANES)),
        ],
        out_specs=pl.BlockSpec((ts, width), lambda i: (i, 0)),
        out_shape=jax.ShapeDtypeStruct((t, width), BF16),
        scratch_shapes=[pltpu.VMEM((ga, halo + ts, LANES), F32), pltpu.VMEM((ga, ts, LANES), F32)],
        compiler_params=_params(("parallel",)),
        name=name,
    )(zg, zg, zg, zg, cw, cb, lg, lb)


def _diff_attn_kernel(qt_ref, kt_ref, q_ref, k_ref, v_ref, lq1_ref, lk1_ref, lq2_ref, lk2_ref, sg_ref,
                      o_ref, qs_scr, vx_scr, m_scr, acc_scr, *, lambda_init, chunk):
    step_id = pl.program_id(1)
    qi = qt_ref[step_id]
    kj = kt_ref[step_id]
    heads, tq, _ = q_ref.shape
    tk = k_ref.shape[1]
    hd = LANES // 2
    scale = hd ** -0.5

    @pl.when(kj == 0)
    def _init():
        lane = lax.broadcasted_iota(jnp.int32, (tq, LANES), 1)
        ones = jnp.ones((tk, LANES), BF16)

        def body(h, carry):
            q = (q_ref[h].astype(F32) * scale).astype(BF16)
            zero = jnp.zeros_like(q)
            qs_scr[h, :tq, :] = jnp.where(lane < hd, q, zero)
            qs_scr[h, tq:, :] = jnp.where(lane >= hd, q, zero)
            vx_scr[h, :, LANES:] = ones
            return carry

        lax.fori_loop(0, heads, body, 0)
        m_scr[...] = jnp.full(m_scr.shape, -jnp.inf, F32)
        acc_scr[...] = jnp.zeros(acc_scr.shape, F32)

    def step(subs):
        def body(h, carry):
            vx_scr[h, :, :LANES] = v_ref[h]
            for sub, masked in subs:
                for r0 in range(0, 2 * tq, chunk):
                    rows = slice(r0, r0 + chunk)
                    q0 = r0 % tq
                    nk = min(tq, -(-(q0 + chunk) // chunk) * chunk) if masked else tq
                    keys = slice(sub * tq, sub * tq + nk)
                    s = lax.dot_general(qs_scr[h, rows, :], k_ref[h, keys, :], NT_DIMS,
                                        preferred_element_type=F32)
                    if masked:
                        row = lax.broadcasted_iota(jnp.int32, (chunk, nk), 0) + q0
                        col = lax.broadcasted_iota(jnp.int32, (chunk, nk), 1)
                        s = jnp.where(col <= row, s, -jnp.inf)
                    m_prev = m_scr[h, rows, :]
                    m_new = jnp.maximum(m_prev, jnp.max(s, axis=-1, keepdims=True))
                    alpha = jnp.exp(m_prev - m_new)
                    p = jnp.exp(s - jnp.concatenate([m_new] * (nk // LANES), axis=1))
                    pv = jnp.dot(p.astype(BF16), vx_scr[h, keys, :], preferred_element_type=F32)
                    acc_scr[h, rows, :] = jnp.concatenate([alpha, alpha], axis=1) * acc_scr[h, rows, :] + pv
                    m_scr[h, rows, :] = m_new
            return carry

        lax.fori_loop(0, heads, body, 0, unroll=True)

    nsub = tk // tq
    last_blk = qi // nsub
    diag_sub = qi % nsub

    @pl.when(kj < last_blk)
    def _off_diagonal():
        step([(sub, False) for sub in range(nsub)])

    for d in range(nsub):
        @pl.when((kj == last_blk) & (diag_sub == d))
        def _diagonal_block():
            step([(sub, False) for sub in range(d)] + [(d, True)])

    @pl.when(kj == last_blk)
    def _finish():
        lam = (jnp.exp(jnp.sum(lq1_ref[...] * lk1_ref[...], axis=-1, keepdims=True))
               - jnp.exp(jnp.sum(lq2_ref[...] * lk2_ref[...], axis=-1, keepdims=True)) + lambda_init)

        for h in range(heads):
            o1 = acc_scr[h, :tq, :LANES] / acc_scr[h, :tq, LANES:]
            o2 = acc_scr[h, tq:, :LANES] / acc_scr[h, tq:, LANES:]
            o = o1 - lam * o2
            ms = jnp.mean(o * o, axis=-1, keepdims=True)
            y = o * lax.rsqrt(ms + EPS) * sg_ref[...]
            o_ref[:, h * LANES:(h + 1) * LANES] = (y * (1.0 - lambda_init)).astype(o_ref.dtype)


def diff_attention(zg, lq1, lk1, lq2, lk2, subln_g, *, batch, seq, a_width, heads, lambda_init, name):
    t = zg.shape[1]
    assert lq1.shape[-1] * 2 == LANES and subln_g.shape[-1] == LANES
    tq = _tile(seq, 512, SUBLANES)
    nq = seq // tq
    first_q = 2 * a_width // LANES
    assert first_q % heads == 0
    qb = first_q // heads
    nsub = 2 if nq % 2 == 0 else 1
    tk = nsub * tq
    nkb = seq // tk
    pairs = [(i, j) for i in range(nq) for j in range(i // nsub + 1)]
    q_tab = jnp.asarray(np.array([p[0] for p in pairs], np.int32))
    k_tab = jnp.asarray(np.array([p[1] for p in pairs], np.int32))
    blk = (heads, tq, LANES)
    kv_blk = (heads, tk, LANES)
    vec = lambda n: pl.BlockSpec((1, n), lambda b, s, qt, kt: (0, 0))
    grid_spec = pltpu.PrefetchScalarGridSpec(
        num_scalar_prefetch=2,
        grid=(batch, len(pairs)),
        in_specs=[
            pl.BlockSpec(blk, lambda b, s, qt, kt: (qb, b * nq + qt[s], 0)),
            pl.BlockSpec(kv_blk, lambda b, s, qt, kt: (qb + 1, b * nkb + kt[s], 0)),
            pl.BlockSpec(kv_blk, lambda b, s, qt, kt: (qb + 2, b * nkb + kt[s], 0)),
            vec(LANES // 2), vec(LANES // 2), vec(LANES // 2), vec(LANES // 2), vec(LANES),
        ],
        out_specs=pl.BlockSpec((tq, heads * LANES), lambda b, s, qt, kt: (b * nq + qt[s], 0)),
        scratch_shapes=[
            pltpu.VMEM((heads, 2 * tq, LANES), BF16),
            pltpu.VMEM((heads, tk, 2 * LANES), BF16),
            pltpu.VMEM((heads, 2 * tq, LANES), F32),
            pltpu.VMEM((heads, 2 * tq, 2 * LANES), F32),
        ],
    )
    return pl.pallas_call(
        functools.partial(_diff_attn_kernel, lambda_init=lambda_init, chunk=_tile(tq, 256, SUBLANES)),
        grid_spec=grid_spec,
        out_shape=jax.ShapeDtypeStruct((t, heads * LANES), BF16),
        compiler_params=_params(("parallel", "arbitrary")),
        name=name,
    )(q_tab, k_tab, zg, zg, zg, lq1[None], lk1[None], lq2[None], lk2[None], subln_g[None])


def _matmul_resid_kernel(*refs):
    *lhs_refs, w_ref, r_ref, out_ref = refs
    acc = r_ref[...]
    k0 = 0
    for lhs_ref in lhs_refs:
        k1 = k0 + lhs_ref.shape[1]
        acc = acc + jnp.dot(lhs_ref[...], w_ref[k0:k1, :], preferred_element_type=F32)
        k0 = k1
    out_ref[...] = acc


def matmul_resid(lhs, w, layer, resid, *, name):
    m = lhs[0].shape[0]
    _, k, n = w.shape
    assert sum(a.shape[1] for a in lhs) == k
    bm = _tile(m, 512, SUBLANES)
    bn = _tile(n, 2048, LANES)
    return pl.pallas_call(
        _matmul_resid_kernel,
        grid=(m // bm, n // bn),
        in_specs=[pl.BlockSpec((bm, a.shape[1]), lambda i, j: (i, 0)) for a in lhs] + [
            pl.BlockSpec((None, k, bn), lambda i, j: (layer, 0, j)),
            pl.BlockSpec((bm, bn), lambda i, j: (i, j)),
        ],
        out_specs=pl.BlockSpec((bm, bn), lambda i, j: (i, j)),
        out_shape=jax.ShapeDtypeStruct((m, n), F32),
        compiler_params=_params(("parallel", "arbitrary")),
        name=name,
    )(*lhs, w, resid)


def _odd_mix_kernel(x_ref, xh_ref, g_ref, wb_ref, wc_ref, wu_ref, cw_ref, o_ref, xn_scr, cu_scr,
                    *, tiles_per_seq, chunk):
    bm = x_ref.shape[0]
    halo = xh_ref.shape[0]
    ksize = cw_ref.shape[0]

    @pl.when(pl.program_id(1) == 0)
    def _():
        _rms_rows_to(xh_ref, g_ref, xn_scr.at[pl.ds(0, halo)])
        _rms_rows_to(x_ref, g_ref, xn_scr.at[pl.ds(halo, bm)])

    keep = jnp.where(pl.program_id(0) % tiles_per_seq == 0, 0.0, 1.0).astype(F32)
    for r0 in range(0, bm, chunk):
        lo = 0 if r0 == 0 else halo + r0
        hi = halo + r0 + chunk
        xs = xn_scr[lo:hi, :]
        cu = (jnp.dot(xs, wc_ref[...], preferred_element_type=F32)
              * jnp.dot(xs, wu_ref[...], preferred_element_type=F32))
        if r0 == 0:
            cu_scr[:halo, :] = cu[:halo] * keep
            cu_scr[halo:hi, :] = cu[halo:]
        else:
            cu_scr[lo:hi, :] = cu
        gate = jnp.dot(xn_scr[halo + r0:hi, :], wb_ref[...], preferred_element_type=F32)
        acc = jnp.zeros_like(gate)
        for j in range(ksize):
            off = halo + r0 - (ksize - 1) + j
            acc = acc + cw_ref[j:j + 1, :] * cu_scr[off:off + chunk, :]
        o_ref[r0:r0 + chunk, :] = (gate * acc).astype(o_ref.dtype)


def odd_mix(x, g, w, layer, conv_w, *, seq, name):
    t, d = x.shape
    ksize, c = conv_w.shape
    assert w.shape[2] == 3 * c
    bm = _tile(seq, 1024, SUBLANES)
    bn = _tile(c, 512, LANES)
    nj = c // bn
    halo = 2 * SUBLANES
    assert ksize - 1 <= halo and bm % halo == 0
    hb = bm // halo
    wspec = lambda part: pl.BlockSpec((None, d, bn), lambda i, j: (layer, 0, part * nj + j))
    return pl.pallas_call(
        functools.partial(_odd_mix_kernel, tiles_per_seq=seq // bm, chunk=_tile(bm, 256, halo)),
        grid=(t // bm, nj),
        in_specs=[
            pl.BlockSpec((bm, d), lambda i, j: (i, 0)),
            pl.BlockSpec((halo, d), lambda i, j: (jnp.maximum(i * hb - 1, 0), 0)),
            pl.BlockSpec((1, d), lambda i, j: (0, 0)),
            wspec(0), wspec(1), wspec(2),
            pl.BlockSpec((ksize, bn), lambda i, j: (0, j)),
        ],
        out_specs=pl.BlockSpec((bm, bn), lambda i, j: (i, j)),
        out_shape=jax.ShapeDtypeStruct((t, c), BF16),
        scratch_shapes=[pltpu.VMEM((halo + bm, d), BF16), pltpu.VMEM((halo + bm, bn), F32)],
        compiler_params=_params(("parallel", "arbitrary")),
        name=name,
    )(x, x, g, w, w, w, conv_w)


def _xattn_mix_kernel(x_ref, g_ref, wq_ref, k_ref, v_ref, o_ref, xn_scr, *, hd):
    @pl.when(pl.program_id(1) == 0)
    def _():
        _rms_rows_to(x_ref, g_ref, xn_scr)

    scale = hd ** -0.5
    for c0 in range(0, wq_ref.shape[1], hd):
        cols = slice(c0, c0 + hd)
        q = jnp.dot(xn_scr[...], wq_ref[:, cols], preferred_element_type=F32).astype(BF16)
        s = lax.dot_general(q, k_ref[:, cols], NT_DIMS, preferred_element_type=F32) * scale
        e = jnp.exp(s - jnp.max(s, axis=-1, keepdims=True))
        p = e / jnp.sum(e, axis=-1, keepdims=True)
        o_ref[:, cols] = jnp.dot(p.astype(BF16), v_ref[:, cols], preferred_element_type=F32).astype(o_ref.dtype)


def xattn_mix(x, g, wq, layer, k_all, v_all, *, seq, mem_len, name):
    t, d = x.shape
    hd = d // X_HEADS
    bm = _tile(seq, 1024, SUBLANES)
    tiles_per_seq = seq // bm
    bn = d
    nj = d // bn
    kv_spec = pl.BlockSpec((mem_len, bn), lambda i, j: (i // tiles_per_seq, layer * nj + j))
    return pl.pallas_call(
        functools.partial(_xattn_mix_kernel, hd=hd),
        grid=(t // bm, nj),
        in_specs=[
            pl.BlockSpec((bm, d), lambda i, j: (i, 0)),
            pl.BlockSpec((1, d), lambda i, j: (0, 0)),
            pl.BlockSpec((None, d, bn), lambda i, j: (layer, 0, j), pipeline_mode=pl.Buffered(1)),
            kv_spec, kv_spec,
        ],
        out_specs=pl.BlockSpec((bm, bn), lambda i, j: (i, j)),
        out_shape=jax.ShapeDtypeStruct((t, d), BF16),
        scratch_shapes=[pltpu.VMEM((bm, d), BF16)],
        compiler_params=_params(("parallel", "arbitrary")),
        name=name,
    )(x, g, wq, k_all, v_all)


def _mlp_kernel(x_ref, att_ref, wo_ref, g_ref, w1_ref, w2_ref, fg_ref, o_ref, xn_ref, *, final_norm):
    j = pl.program_id(1)

    @pl.when(j == 0)
    def _():
        o_ref[...] = x_ref[...] + jnp.dot(att_ref[...], wo_ref[...], preferred_element_type=F32)
        _rms_rows_to(o_ref, g_ref, xn_ref)

    h = jnp.dot(xn_ref[...], w1_ref[...], preferred_element_type=F32)
    h = jnp.square(jnp.maximum(h, 0.0)).astype(BF16)
    o_ref[...] += jnp.dot(h, w2_ref[...], preferred_element_type=F32)

    if final_norm:
        @pl.when(j == pl.num_programs(1) - 1)
        def _():
            _rms_rows_to(o_ref, fg_ref, o_ref)


def mlp(x, att, wo, g, w1, w2, layer, final_g, *, final_norm, name):
    m, d = x.shape
    f = w1.shape[2]
    bm = _tile(m, 512, SUBLANES)
    tf = _tile(f, 1024, LANES)
    return pl.pallas_call(
        functools.partial(_mlp_kernel, final_norm=final_norm),
        grid=(m // bm, f // tf),
        in_specs=[
            pl.BlockSpec((bm, d), lambda i, j: (i, 0)),
            pl.BlockSpec((bm, d), lambda i, j: (i, 0)),
            pl.BlockSpec((None, d, d), lambda i, j: (layer, 0, 0), pipeline_mode=pl.Buffered(1)),
            pl.BlockSpec((1, d), lambda i, j: (0, 0)),
            pl.BlockSpec((None, d, tf), lambda i, j: (layer, 0, j)),
            pl.BlockSpec((None, tf, d), lambda i, j: (layer, j, 0)),
            pl.BlockSpec((1, d), lambda i, j: (0, 0)),
        ],
        out_specs=pl.BlockSpec((bm, d), lambda i, j: (i, 0)),
        out_shape=jax.ShapeDtypeStruct((m, d), F32),
        scratch_shapes=[pltpu.VMEM((bm, d), BF16)],
        compiler_params=_params(("parallel", "arbitrary")),
        name=name,
    )(x, att, wo, g, w1, w2, final_g)


def kernel(x, mem, mem_norm_g, final_norm_g, mix_norm_g, xattn_norm_g, mlp_norm_g, even_w_in, conv_a_w, conv_a_b, ln_a_g, ln_a_b, lambda_q1, lambda_k1, lambda_q2, lambda_k2, subln_g, even_w_out, odd_w_in, conv_c_w, odd_w_out, xq_w, xk_w, xv_w, xo_w, mlp_w1, mlp_w2):
    batch, seq, d = x.shape
    mem_len = mem.shape[1]
    depth = xq_w.shape[0]
    a_width = conv_a_w.shape[-1]
    diff_heads = (d - a_width) // LANES
    t = batch * seq
    bf = lambda a: a.astype(BF16)
    row = lambda v: v[None].astype(F32)

    xf = x.reshape(t, d)
    bm = _tile(seq, 1024, SUBLANES)

    even_w_in, even_w_out, odd_w_in, odd_w_out, xq_w, xk_w, xv_w, xo_w, mlp_w1, mlp_w2 = (
        bf(w) for w in (even_w_in, even_w_out, odd_w_in, odd_w_out, xq_w, xk_w, xv_w, xo_w, mlp_w1, mlp_w2))
    bn_d = _tile(d, 1024, LANES)

    mem_f = mem.reshape(batch * mem_len, d)
    k_all = norm_matmul(mem_f, row(mem_norm_g), xk_w, layers=(0, depth), bm=batch * mem_len, bn=bn_d, name="mem_k")
    v_all = norm_matmul(mem_f, row(mem_norm_g), xv_w, layers=(0, depth), bm=batch * mem_len, bn=bn_d, name="mem_v")

    for l in range(depth):
        if l % 2 == 0:
            e = l // 2
            lambda_init = 0.8 - 0.6 * float(np.exp(-0.3 * l))
            zg = norm_matmul(xf, row(mix_norm_g[l]), even_w_in, layers=(e, 1), bm=bm,
                             bn=_tile(even_w_in.shape[-1], 1024, LANES), group_major=True, name=f"even_in_{l}")
            a = conv_module(zg, conv_a_w[e], conv_a_b[e], ln_a_g[e], ln_a_b[e], seq=seq, name=f"conv_module_{l}")
            o = diff_attention(zg, lambda_q1[e], lambda_k1[e], lambda_q2[e], lambda_k2[e], subln_g[e],
                               batch=batch, seq=seq, a_width=a_width, heads=diff_heads,
                               lambda_init=lambda_init, name=f"diff_attn_{l}")
            xf = matmul_resid([a, o], even_w_out, e, xf, name=f"even_out_{l}")
        else:
            o = l // 2
            y = odd_mix(xf, row(mix_norm_g[l]), odd_w_in, o, conv_c_w[o], seq=seq, name=f"odd_mix_{l}")
            xf = matmul_resid([y], odd_w_out, o, xf, name=f"odd_out_{l}")
        att = xattn_mix(xf, row(xattn_norm_g[l]), xq_w, l, k_all, v_all, seq=seq, mem_len=mem_len,
                        name=f"xattn_mix_{l}")
        xf = mlp(xf, att, xo_w, row(mlp_norm_g[l]), mlp_w1, mlp_w2, l, row(final_norm_g),
                 final_norm=(l == depth - 1), name=f"mlp_{l}")
    return xf.reshape(batch, seq, d)
```

```python
import functools

import numpy as np
import jax
import jax.numpy as jnp
from jax import lax
from jax.experimental import pallas as pl
from jax.experimental.pallas import tpu as pltpu

F32 = jnp.float32
BF16 = jnp.bfloat16
EPS = 1e-6
X_HEADS = 4
LANES = 128
SUBLANES = 8
VMEM_LIMIT_BYTES = 56 * 1024 * 1024
NT_DIMS = (((1,), (1,)), ((), ()))


def _params(semantics):
    return pltpu.CompilerParams(dimension_semantics=semantics, vmem_limit_bytes=VMEM_LIMIT_BYTES)


def _tile(n, target, quantum):
    t = min(n, target) // quantum * quantum
    while t > quantum and n % t:
        t -= quantum
    assert t > 0 and n % t == 0, (n, target, quantum)
    return t


def _rms_rows_to(x_ref, g_ref, dst_ref):
    n = x_ref.shape[0]
    rows = _tile(n, 256, SUBLANES)
    g = g_ref[...]

    def body(c, carry):
        r = pl.ds(pl.multiple_of(c * rows, rows), rows)
        x = x_ref[r, :].astype(F32)
        ms = jnp.mean(x * x, axis=-1, keepdims=True)
        dst_ref[r, :] = (x * lax.rsqrt(ms + EPS) * g).astype(dst_ref.dtype)
        return carry

    lax.fori_loop(0, n // rows, body, 0)


def _norm_matmul_kernel(x_ref, g_ref, w_ref, o_ref, xn_ref, *, group_major):
    @pl.when(pl.program_id(1) == 0)
    def _():
        _rms_rows_to(x_ref, g_ref, xn_ref)

    res = jnp.dot(xn_ref[...], w_ref[...], preferred_element_type=F32)
    if group_major:
        for g in range(o_ref.shape[0]):
            o_ref[g] = res[:, g * LANES:(g + 1) * LANES].astype(o_ref.dtype)
    else:
        o_ref[...] = res.astype(o_ref.dtype)


def norm_matmul(x, g, w, *, layers, bm, bn, group_major=False, name):
    m, d = x.shape
    n = w.shape[2]
    first, nl = layers
    nj = n // bn
    if group_major:
        out_shape = jax.ShapeDtypeStruct((nl * n // LANES, m, LANES), BF16)
        out_spec = pl.BlockSpec((bn // LANES, bm, LANES), lambda i, j: (j, i, 0))
    else:
        out_shape = jax.ShapeDtypeStruct((m, nl * n), BF16)
        out_spec = pl.BlockSpec((bm, bn), lambda i, j: (i, j))
    return pl.pallas_call(
        functools.partial(_norm_matmul_kernel, group_major=group_major),
        grid=(m // bm, nl * nj),
        in_specs=[
            pl.BlockSpec((bm, d), lambda i, j: (i, 0)),
            pl.BlockSpec((1, d), lambda i, j: (0, 0)),
            pl.BlockSpec((None, d, bn), lambda i, j: (first + j // nj, 0, j % nj)),
        ],
        out_specs=out_spec,
        out_shape=out_shape,
        scratch_shapes=[pltpu.VMEM((bm, d), BF16)],
        compiler_params=_params(("parallel", "arbitrary")),
        name=name,
    )(x, g, w)


def _conv_module_kernel(av_ref, ag_ref, avh_ref, agh_ref, cw_ref, cb_ref, lg_ref, lb_ref,
                        o_ref, g_scr, c_scr, *, tiles_per_seq):
    ga, ts, _ = av_ref.shape
    halo = avh_ref.shape[1]
    ksize = cw_ref.shape[1]
    rows = _tile(ts, 128, SUBLANES)
    keep = jnp.where(pl.program_id(0) % tiles_per_seq == 0, 0.0, 1.0).astype(F32)

    def glu(val, gate):
        return val.astype(F32) * jax.nn.sigmoid(gate.astype(F32))

    def conv_group(g, carry):
        g_scr[g, :halo, :] = glu(avh_ref[g], agh_ref[g]) * keep
        g_scr[g, halo:, :] = glu(av_ref[g], ag_ref[g])
        for r0 in range(0, ts, rows):
            acc = jnp.zeros((rows, LANES), F32)
            for j in range(ksize):
                off = r0 + halo - (ksize - 1) + j
                acc = acc + cw_ref[g, j:j + 1, :] * g_scr[g, off:off + rows, :]
            c_scr[g, r0:r0 + rows, :] = acc + cb_ref[g]
        return carry

    lax.fori_loop(0, ga, conv_group, 0)

    width = ga * LANES
    nrows = _tile(ts, 128, SUBLANES)

    def norm_rows(c, carry):
        r = pl.ds(pl.multiple_of(c * nrows, nrows), nrows)
        tot = c_scr[0, r, :]
        for g in range(1, ga):
            tot = tot + c_scr[g, r, :]
        mu = jnp.sum(tot, axis=-1, keepdims=True) / width
        sq = jnp.square(c_scr[0, r, :] - mu)
        for g in range(1, ga):
            sq = sq + jnp.square(c_scr[g, r, :] - mu)
        inv = lax.rsqrt(jnp.sum(sq, axis=-1, keepdims=True) / width + EPS)
        for g in range(ga):
            y = (c_scr[g, r, :] - mu) * inv * lg_ref[g] + lb_ref[g]
            o_ref[r, g * LANES:(g + 1) * LANES] = (y * jax.nn.sigmoid(y)).astype(o_ref.dtype)
        return carry

    lax.fori_loop(0, ts // nrows, norm_rows, 0, unroll=2)


def conv_module(zg, conv_w, conv_b, ln_g, ln_b, *, seq, name):
    ksize, width = conv_w.shape
    ga = width // LANES
    t = zg.shape[1]
    ts = _tile(seq, 512, SUBLANES)
    halo = -(-(ksize - 1) // SUBLANES) * SUBLANES
    assert ts % halo == 0 and width % LANES == 0
    hb = ts // halo

    def per_group(v):
        return jnp.moveaxis(v.reshape(v.shape[:-1] + (ga, LANES)), -2, 0)

    cw = per_group(conv_w)
    cb, lg, lb = (per_group(v[None]) for v in (conv_b, ln_g, ln_b))
    full = lambda shape: pl.BlockSpec(shape, lambda i: (0,) * len(shape))
    return pl.pallas_call(
        functools.partial(_conv_module_kernel, tiles_per_seq=seq // ts),
        grid=(t // ts,),
        in_specs=[
            pl.BlockSpec((ga, ts, LANES), lambda i: (0, i, 0)),
            pl.BlockSpec((ga, ts, LANES), lambda i: (1, i, 0)),
            pl.BlockSpec((ga, halo, LANES), lambda i: (0, jnp.maximum(i * hb - 1, 0), 0)),
            pl.BlockSpec((ga, halo, LANES), lambda i: (1, jnp.maximum(i * hb - 1, 0), 0)),
            full((ga, ksize, LANES)), full((ga, 1, LANES)), full((ga, 1, LANES)), full((ga, 1, LANES)),
        ],
        out_specs=pl.BlockSpec((ts, width), lambda i: (i, 0)),
        out_shape=jax.ShapeDtypeStruct((t, width), BF16),
        scratch_shapes=[pltpu.VMEM((ga, halo + ts, LANES), F32), pltpu.VMEM((ga, ts, LANES), F32)],
        compiler_params=_params(("parallel",)),
        name=name,
    )(zg, zg, zg, zg, cw, cb, lg, lb)


def _diff_attn_kernel(qt_ref, kt_ref, q_ref, k_ref, v_ref, lq1_ref, lk1_ref, lq2_ref, lk2_ref, sg_ref,
                      o_ref, qs_scr, vx_scr, m_scr, acc_scr, *, lambda_init, chunk):
    step_id = pl.program_id(1)
    qi = qt_ref[step_id]
    kj = kt_ref[step_id]
    heads, tq, _ = q_ref.shape
    tk = k_ref.shape[1]
    hd = LANES // 2
    scale = hd ** -0.5

    @pl.when(kj == 0)
    def _init():
        lane = lax.broadcasted_iota(jnp.int32, (tq, LANES), 1)
        ones = jnp.ones((tk, LANES), BF16)

        def body(h, carry):
            q = (q_ref[h].astype(F32) * scale).astype(BF16)
            zero = jnp.zeros_like(q)
            qs_scr[h, :tq, :] = jnp.where(lane < hd, q, zero)
            qs_scr[h, tq:, :] = jnp.where(lane >= hd, q, zero)
            vx_scr[h, :, LANES:] = ones
            return carry

        lax.fori_loop(0, heads, body, 0)
        m_scr[...] = jnp.full(m_scr.shape, -jnp.inf, F32)
        acc_scr[...] = jnp.zeros(acc_scr.shape, F32)

    def step(subs):
        def body(h, carry):
            vx_scr[h, :, :LANES] = v_ref[h]
            for sub, masked in subs:
                for r0 in range(0, 2 * tq, chunk):
                    rows = slice(r0, r0 + chunk)
                    q0 = r0 % tq
                    nk = min(tq, -(-(q0 + chunk) // chunk) * chunk) if masked else tq
                    keys = slice(sub * tq, sub * tq + nk)
                    s = lax.dot_general(qs_scr[h, rows, :], k_ref[h, keys, :], NT_DIMS,
                                        preferred_element_type=F32)
                    if masked:
                        row = lax.broadcasted_iota(jnp.int32, (chunk, nk), 0) + q0
                        col = lax.broadcasted_iota(jnp.int32, (chunk, nk), 1)
                        s = jnp.where(col <= row, s, -jnp.inf)
                    m_prev = m_scr[h, rows, :]
                    m_new = jnp.maximum(m_prev, jnp.max(s, axis=-1, keepdims=True))
                    alpha = jnp.exp(m_prev - m_new)
                    p = jnp.exp(s - jnp.concatenate([m_new] * (nk // LANES), axis=1))
                    pv = jnp.dot(p.astype(BF16), vx_scr[h, keys, :], preferred_element_type=F32)
                    acc_scr[h, rows, :] = jnp.concatenate([alpha, alpha], axis=1) * acc_scr[h, rows, :] + pv
                    m_scr[h, rows, :] = m_new
            return carry

        lax.fori_loop(0, heads, body, 0, unroll=True)

    nsub = tk // tq
    last_blk = qi // nsub
    diag_sub = qi % nsub

    @pl.when(kj < last_blk)
    def _off_diagonal():
        step([(sub, False) for sub in range(nsub)])

    for d in range(nsub):
        @pl.when((kj == last_blk) & (diag_sub == d))
        def _diagonal_block():
            step([(sub, False) for sub in range(d)] + [(d, True)])

    @pl.when(kj == last_blk)
    def _finish():
        lam = (jnp.exp(jnp.sum(lq1_ref[...] * lk1_ref[...], axis=-1, keepdims=True))
               - jnp.exp(jnp.sum(lq2_ref[...] * lk2_ref[...], axis=-1, keepdims=True)) + lambda_init)

        for h in range(heads):
            o1 = acc_scr[h, :tq, :LANES] / acc_scr[h, :tq, LANES:]
            o2 = acc_scr[h, tq:, :LANES] / acc_scr[h, tq:, LANES:]
            o = o1 - lam * o2
            ms = jnp.mean(o * o, axis=-1, keepdims=True)
            y = o * lax.rsqrt(ms + EPS) * sg_ref[...]
            o_ref[:, h * LANES:(h + 1) * LANES] = (y * (1.0 - lambda_init)).astype(o_ref.dtype)


def diff_attention(zg, lq1, lk1, lq2, lk2, subln_g, *, batch, seq, a_width, heads, lambda_init, name):
    t = zg.shape[1]
    assert lq1.shape[-1] * 2 == LANES and subln_g.shape[-1] == LANES
    tq = _tile(seq, 512, SUBLANES)
    nq = seq // tq
    first_q = 2 * a_width // LANES
    assert first_q % heads == 0
    qb = first_q // heads
    nsub = 2 if nq % 2 == 0 else 1
    tk = nsub * tq
    nkb = seq // tk
    pairs = [(i, j) for i in range(nq) for j in range(i // nsub + 1)]
    q_tab = jnp.asarray(np.array([p[0] for p in pairs], np.int32))
    k_tab = jnp.asarray(np.array([p[1] for p in pairs], np.int32))
    blk = (heads, tq, LANES)
    kv_blk = (heads, tk, LANES)
    vec = lambda n: pl.BlockSpec((1, n), lambda b, s, qt, kt: (0, 0))
    grid_spec = pltpu.PrefetchScalarGridSpec(
        num_scalar_prefetch=2,
        grid=(batch, len(pairs)),
        in_specs=[
            pl.BlockSpec(blk, lambda b, s, qt, kt: (qb, b * nq + qt[s], 0)),
            pl.BlockSpec(kv_blk, lambda b, s, qt, kt: (qb + 1, b * nkb + kt[s], 0)),
            pl.BlockSpec(kv_blk, lambda b, s, qt, kt: (qb + 2, b * nkb + kt[s], 0)),
            vec(LANES // 2), vec(LANES // 2), vec(LANES // 2), vec(LANES // 2), vec(LANES),
        ],
        out_specs=pl.BlockSpec((tq, heads * LANES), lambda b, s, qt, kt: (b * nq + qt[s], 0)),
        scratch_shapes=[
            pltpu.VMEM((heads, 2 * tq, LANES), BF16),
            pltpu.VMEM((heads, tk, 2 * LANES), BF16),
            pltpu.VMEM((heads, 2 * tq, LANES), F32),
            pltpu.VMEM((heads, 2 * tq, 2 * LANES), F32),
        ],
    )
    return pl.pallas_call(
        functools.partial(_diff_attn_kernel, lambda_init=lambda_init, chunk=_tile(tq, 256, SUBLANES)),
        grid_spec=grid_spec,
        out_shape=jax.ShapeDtypeStruct((t, heads * LANES), BF16),
        compiler_params=_params(("parallel", "arbitrary")),
        name=name,
    )(q_tab, k_tab, zg, zg, zg, lq1[None], lk1[None], lq2[None], lk2[None], subln_g[None])


def _matmul_resid_kernel(*refs):
    *lhs_refs, w_ref, r_ref, out_ref = refs
    acc = r_ref[...]
    k0 = 0
    for lhs_ref in lhs_refs:
        k1 = k0 + lhs_ref.shape[1]
        acc = acc + jnp.dot(lhs_ref[...], w_ref[k0:k1, :], preferred_element_type=F32)
        k0 = k1
    out_ref[...] = acc


def matmul_resid(lhs, w, layer, resid, *, name):
    m = lhs[0].shape[0]
    _, k, n = w.shape
    assert sum(a.shape[1] for a in lhs) == k
    bm = _tile(m, 512, SUBLANES)
    bn = _tile(n, 2048, LANES)
    return pl.pallas_call(
        _matmul_resid_kernel,
        grid=(m // bm, n // bn),
        in_specs=[pl.BlockSpec((bm, a.shape[1]), lambda i, j: (i, 0)) for a in lhs] + [
            pl.BlockSpec((None, k, bn), lambda i, j: (layer, 0, j)),
            pl.BlockSpec((bm, bn), lambda i, j: (i, j)),
        ],
        out_specs=pl.BlockSpec((bm, bn), lambda i, j: (i, j)),
        out_shape=jax.ShapeDtypeStruct((m, n), F32),
        compiler_params=_params(("parallel", "arbitrary")),
        name=name,
    )(*lhs, w, resid)


def _odd_mix_kernel(x_ref, xh_ref, g_ref, wb_ref, wc_ref, wu_ref, cw_ref, o_ref, xn_scr, cu_scr,
                    *, tiles_per_seq, chunk):
    bm = x_ref.shape[0]
    halo = xh_ref.shape[0]
    ksize = cw_ref.shape[0]

    @pl.when(pl.program_id(1) == 0)
    def _():
        _rms_rows_to(xh_ref, g_ref, xn_scr.at[pl.ds(0, halo)])
        _rms_rows_to(x_ref, g_ref, xn_scr.at[pl.ds(halo, bm)])

    keep = jnp.where(pl.program_id(0) % tiles_per_seq == 0, 0.0, 1.0).astype(F32)
    for r0 in range(0, bm, chunk):
        lo = 0 if r0 == 0 else halo + r0
        hi = halo + r0 + chunk
        xs = xn_scr[lo:hi, :]
        cu = (jnp.dot(xs, wc_ref[...], preferred_element_type=F32)
              * jnp.dot(xs, wu_ref[...], preferred_element_type=F32))
        if r0 == 0:
            cu_scr[:halo, :] = cu[:halo] * keep
            cu_scr[halo:hi, :] = cu[halo:]
        else:
            cu_scr[lo:hi, :] = cu
        gate = jnp.dot(xn_scr[halo + r0:hi, :], wb_ref[...], preferred_element_type=F32)
        acc = jnp.zeros_like(gate)
        for j in range(ksize):
            off = halo + r0 - (ksize - 1) + j
            acc = acc + cw_ref[j:j + 1, :] * cu_scr[off:off + chunk, :]
        o_ref[r0:r0 + chunk, :] = (gate * acc).astype(o_ref.dtype)


def odd_mix(x, g, w, layer, conv_w, *, seq, name):
    t, d = x.shape
    ksize, c = conv_w.shape
    assert w.shape[2] == 3 * c
    bm = _tile(seq, 1024, SUBLANES)
    bn = _tile(c, 512, LANES)
    nj = c // bn
    halo = 2 * SUBLANES
    assert ksize - 1 <= halo and bm % halo == 0
    hb = bm // halo
    wspec = lambda part: pl.BlockSpec((None, d, bn), lambda i, j: (layer, 0, part * nj + j))
    return pl.pallas_call(
        functools.partial(_odd_mix_kernel, tiles_per_seq=seq // bm, chunk=_tile(bm, 256, halo)),
        grid=(t // bm, nj),
        in_specs=[
            pl.BlockSpec((bm, d), lambda i, j: (i, 0)),
            pl.BlockSpec((halo, d), lambda i, j: (jnp.maximum(i * hb - 1, 0), 0)),
            pl.BlockSpec((1, d), lambda i, j: (0, 0)),
            wspec(0), wspec(1), wspec(2),
            pl.BlockSpec((ksize, bn), lambda i, j: (0, j)),
        ],
        out_specs=pl.BlockSpec((bm, bn), lambda i, j: (i, j)),
        out_shape=jax.ShapeDtypeStruct((t, c), BF16),
        scratch_shapes=[pltpu.VMEM((halo + bm, d), BF16), pltpu.VMEM((halo + bm, bn), F32)],
        compiler_params=_params(("parallel", "arbitrary")),
        name=name,
    )(x, x, g, w, w, w, conv_w)


def _xattn_mix_kernel(x_ref, g_ref, wq_ref, k_ref, v_ref, o_ref, xn_scr, *, hd):
    @pl.when(pl.program_id(1) == 0)
    def _():
        _rms_rows_to(x_ref, g_ref, xn_scr)

    scale = hd ** -0.5
    for c0 in range(0, wq_ref.shape[1], hd):
        cols = slice(c0, c0 + hd)
        q = jnp.dot(xn_scr[...], wq_ref[:, cols], preferred_element_type=F32).astype(BF16)
        s = lax.dot_general(q, k_ref[:, cols], NT_DIMS, preferred_element_type=F32) * scale
        e = jnp.exp(s - jnp.max(s, axis=-1, keepdims=True))
        p = e / jnp.sum(e, axis=-1, keepdims=True)
        o_ref[:, cols] = jnp.dot(p.astype(BF16), v_ref[:, cols], preferred_element_type=F32).astype(o_ref.dtype)


def xattn_mix(x, g, wq, layer, k_all, v_all, *, seq, mem_len, name):
    t, d = x.shape
    hd = d // X_HEADS
    bm = _tile(seq, 1024, SUBLANES)
    tiles_per_seq = seq // bm
    bn = d
    nj = d // bn
    kv_spec = pl.BlockSpec((mem_len, bn), lambda i, j: (i // tiles_per_seq, layer * nj + j))
    return pl.pallas_call(
        functools.partial(_xattn_mix_kernel, hd=hd),
        grid=(t // bm, nj),
        in_specs=[
            pl.BlockSpec((bm, d), lambda i, j: (i, 0)),
            pl.BlockSpec((1, d), lambda i, j: (0, 0)),
            pl.BlockSpec((None, d, bn), lambda i, j: (layer, 0, j), pipeline_mode=pl.Buffered(1)),
            kv_spec, kv_spec,
        ],
        out_specs=pl.BlockSpec((bm, bn), lambda i, j: (i, j)),
        out_shape=jax.ShapeDtypeStruct((t, d), BF16),
        scratch_shapes=[pltpu.VMEM((bm, d), BF16)],
        compiler_params=_params(("parallel", "arbitrary")),
        name=name,
    )(x, g, wq, k_all, v_all)


def _mlp_kernel(x_ref, att_ref, wo_ref, g_ref, w1_ref, w2_ref, fg_ref, o_ref, xn_ref, *, final_norm):
    j = pl.program_id(1)

    @pl.when(j == 0)
    def _():
        o_ref[...] = x_ref[...] + jnp.dot(att_ref[...], wo_ref[...], preferred_element_type=F32)
        _rms_rows_to(o_ref, g_ref, xn_ref)

    h = jnp.dot(xn_ref[...], w1_ref[...], preferred_element_type=F32)
    h = jnp.square(jnp.maximum(h, 0.0)).astype(BF16)
    o_ref[...] += jnp.dot(h, w2_ref[...], preferred_element_type=F32)

    if final_norm:
        @pl.when(j == pl.num_programs(1) - 1)
        def _():
            _rms_rows_to(o_ref, fg_ref, o_ref)


def mlp(x, att, wo, g, w1, w2, layer, final_g, *, final_norm, name):
    m, d = x.shape
    f = w1.shape[2]
    bm = _tile(m, 512, SUBLANES)
    tf = _tile(f, 1024, LANES)
    return pl.pallas_call(
        functools.partial(_mlp_kernel, final_norm=final_norm),
        grid=(m // bm, f // tf),
        in_specs=[
            pl.BlockSpec((bm, d), lambda i, j: (i, 0)),
            pl.BlockSpec((bm, d), lambda i, j: (i, 0)),
            pl.BlockSpec((None, d, d), lambda i, j: (layer, 0, 0), pipeline_mode=pl.Buffered(1)),
            pl.BlockSpec((1, d), lambda i, j: (0, 0)),
            pl.BlockSpec((None, d, tf), lambda i, j: (layer, 0, j)),
            pl.BlockSpec((None, tf, d), lambda i, j: (layer, j, 0)),
            pl.BlockSpec((1, d), lambda i, j: (0, 0)),
        ],
        out_specs=pl.BlockSpec((bm, d), lambda i, j: (i, 0)),
        out_shape=jax.ShapeDtypeStruct((m, d), F32),
        scratch_shapes=[pltpu.VMEM((bm, d), BF16)],
        compiler_params=_params(("parallel", "arbitrary")),
        name=name,
    )(x, att, wo, g, w1, w2, final_g)


def kernel(x, mem, mem_norm_g, final_norm_g, mix_norm_g, xattn_norm_g, mlp_norm_g, even_w_in, conv_a_w, conv_a_b, ln_a_g, ln_a_b, lambda_q1, lambda_k1, lambda_q2, lambda_k2, subln_g, even_w_out, odd_w_in, conv_c_w, odd_w_out, xq_w, xk_w, xv_w, xo_w, mlp_w1, mlp_w2):
    batch, seq, d = x.shape
    mem_len = mem.shape[1]
    depth = xq_w.shape[0]
    a_width = conv_a_w.shape[-1]
    diff_heads = (d - a_width) // LANES
    t = batch * seq
    bf = lambda a: a.astype(BF16)
    row = lambda v: v[None].astype(F32)

    xf = x.reshape(t, d)
    bm = _tile(seq, 1024, SUBLANES)

    even_w_in, even_w_out, odd_w_in, odd_w_out, xq_w, xk_w, xv_w, xo_w, mlp_w1, mlp_w2 = (
        bf(w) for w in (even_w_in, even_w_out, odd_w_in, odd_w_out, xq_w, xk_w, xv_w, xo_w, mlp_w1, mlp_w2))
    bn_d = _tile(d, 1024, LANES)

    mem_f = mem.reshape(batch * mem_len, d)
    k_all = norm_matmul(mem_f, row(mem_norm_g), xk_w, layers=(0, depth), bm=batch * mem_len, bn=bn_d, name="mem_k")
    v_all = norm_matmul(mem_f, row(mem_norm_g), xv_w, layers=(0, depth), bm=batch * mem_len, bn=bn_d, name="mem_v")

    for l in range(depth):
        if l % 2 == 0:
            e = l // 2
            lambda_init = 0.8 - 0.6 * float(np.exp(-0.3 * l))
            zg = norm_matmul(xf, row(mix_norm_g[l]), even_w_in, layers=(e, 1), bm=bm,
                             bn=_tile(even_w_in.shape[-1], 1280, LANES), group_major=True, name=f"even_in_{l}")
            a = conv_module(zg, conv_a_w[e], conv_a_b[e], ln_a_g[e], ln_a_b[e], seq=seq, name=f"conv_module_{l}")
            o = diff_attention(zg, lambda_q1[e], lambda_k1[e], lambda_q2[e], lambda_k2[e], subln_g[e],
                               batch=batch, seq=seq, a_width=a_width, heads=diff_heads,
                               lambda_init=lambda_init, name=f"diff_attn_{l}")
            xf = matmul_resid([a, o], even_w_out, e, xf, name=f"even_out_{l}")
        else:
            o = l // 2
            y = odd_mix(xf, row(mix_norm_g[l]), odd_w_in, o, conv_c_w[o], seq=seq, name=f"odd_mix_{l}")
            xf = matmul_resid([y], odd_w_out, o, xf, name=f"odd_out_{l}")
        att = xattn_mix(xf, row(xattn_norm_g[l]), xq_w, l, k_all, v_all, seq=seq, mem_len=mem_len,
                        name=f"xattn_mix_{l}")
        xf = mlp(xf, att, xo_w, row(mlp_norm_g[l]), mlp_w1, mlp_w2, l, row(final_norm_g),
                 final_norm=(l == depth - 1), name=f"mlp_{l}")
    return xf.reshape(batch, seq, d)
```

```python
import functools

import numpy as np
import jax
import jax.numpy as jnp
from jax import lax
from jax.experimental import pallas as pl
from jax.experimental.pallas import tpu as pltpu

F32 = jnp.float32
BF16 = jnp.bfloat16
EPS = 1e-6
X_HEADS = 4
LANES = 128
SUBLANES = 8
VMEM_LIMIT_BYTES = 56 * 1024 * 1024
NT_DIMS = (((1,), (1,)), ((), ()))


def _params(semantics):
    return pltpu.CompilerParams(dimension_semantics=semantics, vmem_limit_bytes=VMEM_LIMIT_BYTES)


def _tile(n, target, quantum):
    t = min(n, target) // quantum * quantum
    while t > quantum and n % t:
        t -= quantum
    assert t > 0 and n % t == 0, (n, target, quantum)
    return t


def _rms_rows_to(x_ref, g_ref, dst_ref):
    n = x_ref.shape[0]
    rows = _tile(n, 256, SUBLANES)
    g = g_ref[...]

    def body(c, carry):
        r = pl.ds(pl.multiple_of(c * rows, rows), rows)
        x = x_ref[r, :].astype(F32)
        ms = jnp.mean(x * x, axis=-1, keepdims=True)
        dst_ref[r, :] = (x * lax.rsqrt(ms + EPS) * g).astype(dst_ref.dtype)
        return carry

    lax.fori_loop(0, n // rows, body, 0)


def _norm_matmul_kernel(x_ref, g_ref, w_ref, o_ref, xn_ref, *, group_major):
    @pl.when(pl.program_id(1) == 0)
    def _():
        _rms_rows_to(x_ref, g_ref, xn_ref)

    w = w_ref[...]
    if w.dtype != BF16:
        w = w.astype(BF16)
    res = jnp.dot(xn_ref[...], w, preferred_element_type=F32)
    if group_major:
        for g in range(o_ref.shape[0]):
            o_ref[g] = res[:, g * LANES:(g + 1) * LANES].astype(o_ref.dtype)
    else:
        o_ref[...] = res.astype(o_ref.dtype)


def norm_matmul(x, g, w, *, layers, bm, bn, group_major=False, name):
    m, d = x.shape
    n = w.shape[2]
    first, nl = layers
    nj = n // bn
    if group_major:
        out_shape = jax.ShapeDtypeStruct((nl * n // LANES, m, LANES), BF16)
        out_spec = pl.BlockSpec((bn // LANES, bm, LANES), lambda i, j: (j, i, 0))
    else:
        out_shape = jax.ShapeDtypeStruct((m, nl * n), BF16)
        out_spec = pl.BlockSpec((bm, bn), lambda i, j: (i, j))
    return pl.pallas_call(
        functools.partial(_norm_matmul_kernel, group_major=group_major),
        grid=(m // bm, nl * nj),
        in_specs=[
            pl.BlockSpec((bm, d), lambda i, j: (i, 0)),
            pl.BlockSpec((1, d), lambda i, j: (0, 0)),
            pl.BlockSpec((None, d, bn), lambda i, j: (first + j // nj, 0, j % nj)),
        ],
        out_specs=out_spec,
        out_shape=out_shape,
        scratch_shapes=[pltpu.VMEM((bm, d), BF16)],
        compiler_params=_params(("parallel", "arbitrary")),
        name=name,
    )(x, g, w)


def _conv_module_kernel(av_ref, ag_ref, avh_ref, agh_ref, cw_ref, cb_ref, lg_ref, lb_ref,
                        o_ref, g_scr, c_scr, *, tiles_per_seq):
    ga, ts, _ = av_ref.shape
    halo = avh_ref.shape[1]
    ksize = cw_ref.shape[1]
    rows = _tile(ts, 128, SUBLANES)
    keep = jnp.where(pl.program_id(0) % tiles_per_seq == 0, 0.0, 1.0).astype(F32)

    def glu(val, gate):
        return val.astype(F32) * jax.nn.sigmoid(gate.astype(F32))

    def conv_group(g, carry):
        g_scr[g, :halo, :] = glu(avh_ref[g], agh_ref[g]) * keep
        g_scr[g, halo:, :] = glu(av_ref[g], ag_ref[g])
        for r0 in range(0, ts, rows):
            acc = jnp.zeros((rows, LANES), F32)
            for j in range(ksize):
                off = r0 + halo - (ksize - 1) + j
                acc = acc + cw_ref[g, j:j + 1, :] * g_scr[g, off:off + rows, :]
            c_scr[g, r0:r0 + rows, :] = acc + cb_ref[g]
        return carry

    lax.fori_loop(0, ga, conv_group, 0)

    width = ga * LANES
    nrows = _tile(ts, 128, SUBLANES)

    def norm_rows(c, carry):
        r = pl.ds(pl.multiple_of(c * nrows, nrows), nrows)
        tot = c_scr[0, r, :]
        for g in range(1, ga):
            tot = tot + c_scr[g, r, :]
        mu = jnp.sum(tot, axis=-1, keepdims=True) / width
        sq = jnp.square(c_scr[0, r, :] - mu)
        for g in range(1, ga):
            sq = sq + jnp.square(c_scr[g, r, :] - mu)
        inv = lax.rsqrt(jnp.sum(sq, axis=-1, keepdims=True) / width + EPS)
        for g in range(ga):
            y = (c_scr[g, r, :] - mu) * inv * lg_ref[g] + lb_ref[g]
            o_ref[r, g * LANES:(g + 1) * LANES] = (y * jax.nn.sigmoid(y)).astype(o_ref.dtype)
        return carry

    lax.fori_loop(0, ts // nrows, norm_rows, 0, unroll=2)


def conv_module(zg, conv_w, conv_b, ln_g, ln_b, *, seq, name):
    ksize, width = conv_w.shape
    ga = width // LANES
    t = zg.shape[1]
    ts = _tile(seq, 512, SUBLANES)
    halo = -(-(ksize - 1) // SUBLANES) * SUBLANES
    assert ts % halo == 0 and width % LANES == 0
    hb = ts // halo

    def per_group(v):
        return jnp.moveaxis(v.reshape(v.shape[:-1] + (ga, LANES)), -2, 0)

    cw = per_group(conv_w)
    cb, lg, lb = (per_group(v[None]) for v in (conv_b, ln_g, ln_b))
    full = lambda shape: pl.BlockSpec(shape, lambda i: (0,) * len(shape))
    return pl.pallas_call(
        functools.partial(_conv_module_kernel, tiles_per_seq=seq // ts),
        grid=(t // ts,),
        in_specs=[
            pl.BlockSpec((ga, ts, LANES), lambda i: (0, i, 0)),
            pl.BlockSpec((ga, ts, LANES), lambda i: (1, i, 0)),
            pl.BlockSpec((ga, halo, LANES), lambda i: (0, jnp.maximum(i * hb - 1, 0), 0)),
            pl.BlockSpec((ga, halo, LANES), lambda i: (1, jnp.maximum(i * hb - 1, 0), 0)),
            full((ga, ksize, LANES)), full((ga, 1, LANES)), full((ga, 1, LANES)), full((ga, 1, LANES)),
        ],
        out_specs=pl.BlockSpec((ts, width), lambda i: (i, 0)),
        out_shape=jax.ShapeDtypeStruct((t, width), BF16),
        scratch_shapes=[pltpu.VMEM((ga, halo + ts, LANES), F32), pltpu.VMEM((ga, ts, LANES), F32)],
        compiler_params=_params(("parallel",)),
        name=name,
    )(zg, zg, zg, zg, cw, cb, lg, lb)


def _diff_attn_kernel(qt_ref, kt_ref, q_ref, k_ref, v_ref, lq1_ref, lk1_ref, lq2_ref, lk2_ref, sg_ref,
                      o_ref, qs_scr, vx_scr, m_scr, acc_scr, *, lambda_init, chunk):
    step_id = pl.program_id(1)
    qi = qt_ref[step_id]
    kj = kt_ref[step_id]
    heads, tq, _ = q_ref.shape
    tk = k_ref.shape[1]
    hd = LANES // 2
    scale = hd ** -0.5

    @pl.when(kj == 0)
    def _init():
        lane = lax.broadcasted_iota(jnp.int32, (tq, LANES), 1)
        ones = jnp.ones((tk, LANES), BF16)

        def body(h, carry):
            q = (q_ref[h].astype(F32) * scale).astype(BF16)
            zero = jnp.zeros_like(q)
            qs_scr[h, :tq, :] = jnp.where(lane < hd, q, zero)
            qs_scr[h, tq:, :] = jnp.where(lane >= hd, q, zero)
            vx_scr[h, :, LANES:] = ones
            return carry

        lax.fori_loop(0, heads, body, 0)
        m_scr[...] = jnp.full(m_scr.shape, -jnp.inf, F32)
        acc_scr[...] = jnp.zeros(acc_scr.shape, F32)

    def step(subs):
        def body(h, carry):
            vx_scr[h, :, :LANES] = v_ref[h]
            for sub, masked in subs:
                for r0 in range(0, 2 * tq, chunk):
                    rows = slice(r0, r0 + chunk)
                    q0 = r0 % tq
                    nk = min(tq, -(-(q0 + chunk) // chunk) * chunk) if masked else tq
                    keys = slice(sub * tq, sub * tq + nk)
                    s = lax.dot_general(qs_scr[h, rows, :], k_ref[h, keys, :], NT_DIMS,
                                        preferred_element_type=F32)
                    if masked:
                        row = lax.broadcasted_iota(jnp.int32, (chunk, nk), 0) + q0
                        col = lax.broadcasted_iota(jnp.int32, (chunk, nk), 1)
                        s = jnp.where(col <= row, s, -jnp.inf)
                    m_prev = m_scr[h, rows, :]
                    m_new = jnp.maximum(m_prev, jnp.max(s, axis=-1, keepdims=True))
                    alpha = jnp.exp(m_prev - m_new)
                    p = jnp.exp(s - jnp.concatenate([m_new] * (nk // LANES), axis=1))
                    pv = jnp.dot(p.astype(BF16), vx_scr[h, keys, :], preferred_element_type=F32)
                    acc_scr[h, rows, :] = jnp.concatenate([alpha, alpha], axis=1) * acc_scr[h, rows, :] + pv
                    m_scr[h, rows, :] = m_new
            return carry

        lax.fori_loop(0, heads, body, 0, unroll=True)

    nsub = tk // tq
    last_blk = qi // nsub
    diag_sub = qi % nsub

    @pl.when(kj < last_blk)
    def _off_diagonal():
        step([(sub, False) for sub in range(nsub)])

    for d in range(nsub):
        @pl.when((kj == last_blk) & (diag_sub == d))
        def _diagonal_block():
            step([(sub, False) for sub in range(d)] + [(d, True)])

    @pl.when(kj == last_blk)
    def _finish():
        lam = (jnp.exp(jnp.sum(lq1_ref[...] * lk1_ref[...], axis=-1, keepdims=True))
               - jnp.exp(jnp.sum(lq2_ref[...] * lk2_ref[...], axis=-1, keepdims=True)) + lambda_init)

        for h in range(heads):
            o1 = acc_scr[h, :tq, :LANES] / acc_scr[h, :tq, LANES:]
            o2 = acc_scr[h, tq:, :LANES] / acc_scr[h, tq:, LANES:]
            o = o1 - lam * o2
            ms = jnp.mean(o * o, axis=-1, keepdims=True)
            y = o * lax.rsqrt(ms + EPS) * sg_ref[...]
            o_ref[:, h * LANES:(h + 1) * LANES] = (y * (1.0 - lambda_init)).astype(o_ref.dtype)


def diff_attention(zg, lq1, lk1, lq2, lk2, subln_g, *, batch, seq, a_width, heads, lambda_init, name):
    t = zg.shape[1]
    assert lq1.shape[-1] * 2 == LANES and subln_g.shape[-1] == LANES
    tq = _tile(seq, 512, SUBLANES)
    nq = seq // tq
    first_q = 2 * a_width // LANES
    assert first_q % heads == 0
    qb = first_q // heads
    nsub = 2 if nq % 2 == 0 else 1
    tk = nsub * tq
    nkb = seq // tk
    pairs = [(i, j) for i in range(nq) for j in range(i // nsub + 1)]
    q_tab = jnp.asarray(np.array([p[0] for p in pairs], np.int32))
    k_tab = jnp.asarray(np.array([p[1] for p in pairs], np.int32))
    blk = (heads, tq, LANES)
    kv_blk = (heads, tk, LANES)
    vec = lambda n: pl.BlockSpec((1, n), lambda b, s, qt, kt: (0, 0))
    grid_spec = pltpu.PrefetchScalarGridSpec(
        num_scalar_prefetch=2,
        grid=(batch, len(pairs)),
        in_specs=[
            pl.BlockSpec(blk, lambda b, s, qt, kt: (qb, b * nq + qt[s], 0)),
            pl.BlockSpec(kv_blk, lambda b, s, qt, kt: (qb + 1, b * nkb + kt[s], 0)),
            pl.BlockSpec(kv_blk, lambda b, s, qt, kt: (qb + 2, b * nkb + kt[s], 0)),
            vec(LANES // 2), vec(LANES // 2), vec(LANES // 2), vec(LANES // 2), vec(LANES),
        ],
        out_specs=pl.BlockSpec((tq, heads * LANES), lambda b, s, qt, kt: (b * nq + qt[s], 0)),
        scratch_shapes=[
            pltpu.VMEM((heads, 2 * tq, LANES), BF16),
            pltpu.VMEM((heads, tk, 2 * LANES), BF16),
            pltpu.VMEM((heads, 2 * tq, LANES), F32),
            pltpu.VMEM((heads, 2 * tq, 2 * LANES), F32),
        ],
    )
    return pl.pallas_call(
        functools.partial(_diff_attn_kernel, lambda_init=lambda_init, chunk=_tile(tq, 256, SUBLANES)),
        grid_spec=grid_spec,
        out_shape=jax.ShapeDtypeStruct((t, heads * LANES), BF16),
        compiler_params=_params(("parallel", "arbitrary")),
        name=name,
    )(q_tab, k_tab, zg, zg, zg, lq1[None], lk1[None], lq2[None], lk2[None], subln_g[None])


def _matmul_resid_kernel(*refs):
    *lhs_refs, w_ref, r_ref, out_ref, w_scr = refs

    @pl.when(pl.program_id(0) == 0)
    def _():
        rows = _tile(w_ref.shape[0], 256, SUBLANES)

        def body(c, carry):
            r = pl.ds(pl.multiple_of(c * rows, rows), rows)
            w_scr[r, :] = w_ref[r, :].astype(w_scr.dtype)
            return carry

        lax.fori_loop(0, w_ref.shape[0] // rows, body, 0)

    acc = r_ref[...]
    k0 = 0
    for lhs_ref in lhs_refs:
        k1 = k0 + lhs_ref.shape[1]
        acc = acc + jnp.dot(lhs_ref[...], w_scr[k0:k1, :], preferred_element_type=F32)
        k0 = k1
    out_ref[...] = acc


def matmul_resid(lhs, w, layer, resid, *, name):
    m = lhs[0].shape[0]
    _, k, n = w.shape
    assert sum(a.shape[1] for a in lhs) == k
    bm = _tile(m, 512, SUBLANES)
    return pl.pallas_call(
        _matmul_resid_kernel,
        grid=(m // bm,),
        in_specs=[pl.BlockSpec((bm, a.shape[1]), lambda i: (i, 0)) for a in lhs] + [
            pl.BlockSpec((None, k, n), lambda i: (layer, 0, 0), pipeline_mode=pl.Buffered(1)),
            pl.BlockSpec((bm, n), lambda i: (i, 0)),
        ],
        out_specs=pl.BlockSpec((bm, n), lambda i: (i, 0)),
        out_shape=jax.ShapeDtypeStruct((m, n), F32),
        scratch_shapes=[pltpu.VMEM((k, n), BF16)],
        compiler_params=_params(("arbitrary",)),
        name=name,
    )(*lhs, w, resid)


def _odd_mix_kernel(x_ref, xh_ref, g_ref, wb_ref, wc_ref, wu_ref, cw_ref, o_ref, xn_scr, cu_scr,
                    *, tiles_per_seq, chunk):
    bm = x_ref.shape[0]
    halo = xh_ref.shape[0]
    ksize = cw_ref.shape[0]

    @pl.when(pl.program_id(1) == 0)
    def _():
        _rms_rows_to(xh_ref, g_ref, xn_scr.at[pl.ds(0, halo)])
        _rms_rows_to(x_ref, g_ref, xn_scr.at[pl.ds(halo, bm)])

    keep = jnp.where(pl.program_id(0) % tiles_per_seq == 0, 0.0, 1.0).astype(F32)
    for r0 in range(0, bm, chunk):
        lo = 0 if r0 == 0 else halo + r0
        hi = halo + r0 + chunk
        xs = xn_scr[lo:hi, :]
        cu = (jnp.dot(xs, wc_ref[...], preferred_element_type=F32)
              * jnp.dot(xs, wu_ref[...], preferred_element_type=F32))
        if r0 == 0:
            cu_scr[:halo, :] = cu[:halo] * keep
            cu_scr[halo:hi, :] = cu[halo:]
        else:
            cu_scr[lo:hi, :] = cu
        gate = jnp.dot(xn_scr[halo + r0:hi, :], wb_ref[...], preferred_element_type=F32)
        acc = jnp.zeros_like(gate)
        for j in range(ksize):
            off = halo + r0 - (ksize - 1) + j
            acc = acc + cw_ref[j:j + 1, :] * cu_scr[off:off + chunk, :]
        o_ref[r0:r0 + chunk, :] = (gate * acc).astype(o_ref.dtype)


def odd_mix(x, g, w, layer, conv_w, *, seq, name):
    t, d = x.shape
    ksize, c = conv_w.shape
    assert w.shape[2] == 3 * c
    bm = _tile(seq, 1024, SUBLANES)
    bn = _tile(c, 512, LANES)
    nj = c // bn
    halo = 2 * SUBLANES
    assert ksize - 1 <= halo and bm % halo == 0
    hb = bm // halo
    wspec = lambda part: pl.BlockSpec((None, d, bn), lambda i, j: (layer, 0, part * nj + j))
    return pl.pallas_call(
        functools.partial(_odd_mix_kernel, tiles_per_seq=seq // bm, chunk=_tile(bm, 256, halo)),
        grid=(t // bm, nj),
        in_specs=[
            pl.BlockSpec((bm, d), lambda i, j: (i, 0)),
            pl.BlockSpec((halo, d), lambda i, j: (jnp.maximum(i * hb - 1, 0), 0)),
            pl.BlockSpec((1, d), lambda i, j: (0, 0)),
            wspec(0), wspec(1), wspec(2),
            pl.BlockSpec((ksize, bn), lambda i, j: (0, j)),
        ],
        out_specs=pl.BlockSpec((bm, bn), lambda i, j: (i, j)),
        out_shape=jax.ShapeDtypeStruct((t, c), BF16),
        scratch_shapes=[pltpu.VMEM((halo + bm, d), BF16), pltpu.VMEM((halo + bm, bn), F32)],
        compiler_params=_params(("parallel", "arbitrary")),
        name=name,
    )(x, x, g, w, w, w, conv_w)


def _xattn_mix_kernel(x_ref, g_ref, wq_ref, k_ref, v_ref, o_ref, xn_scr, *, hd):
    @pl.when(pl.program_id(1) == 0)
    def _():
        _rms_rows_to(x_ref, g_ref, xn_scr)

    scale = hd ** -0.5
    for c0 in range(0, wq_ref.shape[1], hd):
        cols = slice(c0, c0 + hd)
        q = jnp.dot(xn_scr[...], wq_ref[:, cols], preferred_element_type=F32).astype(BF16)
        s = lax.dot_general(q, k_ref[:, cols], NT_DIMS, preferred_element_type=F32) * scale
        e = jnp.exp(s - jnp.max(s, axis=-1, keepdims=True))
        p = e / jnp.sum(e, axis=-1, keepdims=True)
        o_ref[:, cols] = jnp.dot(p.astype(BF16), v_ref[:, cols], preferred_element_type=F32).astype(o_ref.dtype)


def xattn_mix(x, g, wq, layer, k_all, v_all, *, seq, mem_len, name):
    t, d = x.shape
    hd = d // X_HEADS
    bm = _tile(seq, 1024, SUBLANES)
    tiles_per_seq = seq // bm
    bn = d
    nj = d // bn
    kv_spec = pl.BlockSpec((mem_len, bn), lambda i, j: (i // tiles_per_seq, layer * nj + j))
    return pl.pallas_call(
        functools.partial(_xattn_mix_kernel, hd=hd),
        grid=(t // bm, nj),
        in_specs=[
            pl.BlockSpec((bm, d), lambda i, j: (i, 0)),
            pl.BlockSpec((1, d), lambda i, j: (0, 0)),
            pl.BlockSpec((None, d, bn), lambda i, j: (layer, 0, j), pipeline_mode=pl.Buffered(1)),
            kv_spec, kv_spec,
        ],
        out_specs=pl.BlockSpec((bm, bn), lambda i, j: (i, j)),
        out_shape=jax.ShapeDtypeStruct((t, d), BF16),
        scratch_shapes=[pltpu.VMEM((bm, d), BF16)],
        compiler_params=_params(("parallel", "arbitrary")),
        name=name,
    )(x, g, wq, k_all, v_all)


def _mlp_kernel(x_ref, att_ref, wo_ref, g_ref, w1_ref, w2_ref, fg_ref, o_ref, xn_ref, *, final_norm):
    j = pl.program_id(1)

    @pl.when(j == 0)
    def _():
        o_ref[...] = x_ref[...] + jnp.dot(att_ref[...], wo_ref[...], preferred_element_type=F32)
        _rms_rows_to(o_ref, g_ref, xn_ref)

    h = jnp.dot(xn_ref[...], w1_ref[...], preferred_element_type=F32)
    h = jnp.square(jnp.maximum(h, 0.0)).astype(BF16)
    o_ref[...] += jnp.dot(h, w2_ref[...], preferred_element_type=F32)

    if final_norm:
        @pl.when(j == pl.num_programs(1) - 1)
        def _():
            _rms_rows_to(o_ref, fg_ref, o_ref)


def mlp(x, att, wo, g, w1, w2, layer, final_g, *, final_norm, name):
    m, d = x.shape
    f = w1.shape[2]
    bm = _tile(m, 512, SUBLANES)
    tf = _tile(f, 1024, LANES)
    return pl.pallas_call(
        functools.partial(_mlp_kernel, final_norm=final_norm),
        grid=(m // bm, f // tf),
        in_specs=[
            pl.BlockSpec((bm, d), lambda i, j: (i, 0)),
            pl.BlockSpec((bm, d), lambda i, j: (i, 0)),
            pl.BlockSpec((None, d, d), lambda i, j: (layer, 0, 0), pipeline_mode=pl.Buffered(1)),
            pl.BlockSpec((1, d), lambda i, j: (0, 0)),
            pl.BlockSpec((None, d, tf), lambda i, j: (layer, 0, j)),
            pl.BlockSpec((None, tf, d), lambda i, j: (layer, j, 0)),
            pl.BlockSpec((1, d), lambda i, j: (0, 0)),
        ],
        out_specs=pl.BlockSpec((bm, d), lambda i, j: (i, 0)),
        out_shape=jax.ShapeDtypeStruct((m, d), F32),
        scratch_shapes=[pltpu.VMEM((bm, d), BF16)],
        compiler_params=_params(("parallel", "arbitrary")),
        name=name,
    )(x, att, wo, g, w1, w2, final_g)


def kernel(x, mem, mem_norm_g, final_norm_g, mix_norm_g, xattn_norm_g, mlp_norm_g, even_w_in, conv_a_w, conv_a_b, ln_a_g, ln_a_b, lambda_q1, lambda_k1, lambda_q2, lambda_k2, subln_g, even_w_out, odd_w_in, conv_c_w, odd_w_out, xq_w, xk_w, xv_w, xo_w, mlp_w1, mlp_w2):
    batch, seq, d = x.shape
    mem_len = mem.shape[1]
    depth = xq_w.shape[0]
    a_width = conv_a_w.shape[-1]
    diff_heads = (d - a_width) // LANES
    t = batch * seq
    bf = lambda a: a.astype(BF16)
    row = lambda v: v[None].astype(F32)

    xf = x.reshape(t, d)
    bm = _tile(seq, 1024, SUBLANES)

    even_w_in, odd_w_in, xq_w, xo_w, mlp_w1, mlp_w2 = (
        bf(w) for w in (even_w_in, odd_w_in, xq_w, xo_w, mlp_w1, mlp_w2))
    bn_d = _tile(d, 1024, LANES)

    mem_f = mem.reshape(batch * mem_len, d)
    k_all = norm_matmul(mem_f, row(mem_norm_g), xk_w, layers=(0, depth), bm=batch * mem_len, bn=bn_d, name="mem_k")
    v_all = norm_matmul(mem_f, row(mem_norm_g), xv_w, layers=(0, depth), bm=batch * mem_len, bn=bn_d, name="mem_v")

    for l in range(depth):
        if l % 2 == 0:
            e = l // 2
            lambda_init = 0.8 - 0.6 * float(np.exp(-0.3 * l))
            zg = norm_matmul(xf, row(mix_norm_g[l]), even_w_in, layers=(e, 1), bm=bm,
                             bn=_tile(even_w_in.shape[-1], 1280, LANES), group_major=True, name=f"even_in_{l}")
            a = conv_module(zg, conv_a_w[e], conv_a_b[e], ln_a_g[e], ln_a_b[e], seq=seq, name=f"conv_module_{l}")
            o = diff_attention(zg, lambda_q1[e], lambda_k1[e], lambda_q2[e], lambda_k2[e], subln_g[e],
                               batch=batch, seq=seq, a_width=a_width, heads=diff_heads,
                               lambda_init=lambda_init, name=f"diff_attn_{l}")
            xf = matmul_resid([a, o], even_w_out, e, xf, name=f"even_out_{l}")
        else:
            o = l // 2
            y = odd_mix(xf, row(mix_norm_g[l]), odd_w_in, o, conv_c_w[o], seq=seq, name=f"odd_mix_{l}")
            xf = matmul_resid([y], odd_w_out, o, xf, name=f"odd_out_{l}")
        att = xattn_mix(xf, row(xattn_norm_g[l]), xq_w, l, k_all, v_all, seq=seq, mem_len=mem_len,
                        name=f"xattn_mix_{l}")
        xf = mlp(xf, att, xo_w, row(mlp_norm_g[l]), mlp_w1, mlp_w2, l, row(final_norm_g),
                 final_norm=(l == depth - 1), name=f"mlp_{l}")
    return xf.reshape(batch, seq, d)
```
